```python
import math
import jax
import jax.numpy as jnp
from jax import lax
import numpy as np

D_MODEL = 1024
BATCH = 2
SEQ = 16384
DEPTH = 2

GRID_W = 64
CTX_LEN = 256
HEAD_DIM = 64
MIX_WIDTH = D_MODEL
GROUP_WIDTH = MIX_WIDTH // 2
N_Q_HEADS = GROUP_WIDTH // HEAD_DIM
N_KV_HEADS = N_Q_HEADS // 4
KV_WIDTH = N_KV_HEADS * HEAD_DIM
HYENA_WIDTH = GROUP_WIDTH
HYENA_ORDER = 2
HYENA_SHORT_CONV = 3
HYENA_EMB = 33
HYENA_HIDDEN = 64
HYENA_DECAY_TARGET = 1e-2
HYENA_SHORT_DECAY_PCT = 0.3
HYENA_LONG_DECAY_PCT = 1.5
Q_BLOCK = 128
WINDOW = 128
NA_ROWS = 8
NA_COLS = 16
ROPE_THETA = 10000.0
EPS = 1e-6
NEG_INF = -1e30
N_EVEN = (DEPTH + 1) // 2
N_ODD = DEPTH // 2
EVEN_SPLITS = (GROUP_WIDTH, KV_WIDTH, KV_WIDTH, (HYENA_ORDER + 1) * HYENA_WIDTH, GROUP_WIDTH, HYENA_WIDTH)
ODD_SPLITS = (GROUP_WIDTH, KV_WIDTH, KV_WIDTH, GROUP_WIDTH, GROUP_WIDTH, GROUP_WIDTH, GROUP_WIDTH, GROUP_WIDTH)
IN_WIDTH = sum(EVEN_SPLITS)

kernel_name = "hybrid_flow_backbone"


def rms_norm(x, g):
    xf = x.astype(jnp.float32)
    y = xf * lax.rsqrt(jnp.mean(xf * xf, axis=-1, keepdims=True) + EPS)
    return (y * g.astype(jnp.float32)).astype(x.dtype)


def adaln(cond, w, b):
    m = jnp.matmul(jax.nn.silu(cond), w) + b
    return jnp.split(m, 3, axis=-1)


def modulated_norm(x, g, shift, scale):
    return rms_norm(x, g) * (1 + scale) + shift


def split_cols(p, sizes):
    idx = [int(v) for v in np.cumsum(sizes)[:-1]]
    return jnp.split(p, idx, axis=-1)


def heads(t, n):
    return t.reshape(t.shape[0], t.shape[1], n, HEAD_DIM)


def axial_rope_tables(n_tokens):
    t = jnp.arange(n_tokens, dtype=jnp.int32)
    row = (t // GRID_W).astype(jnp.float32)
    col = (t % GRID_W).astype(jnp.float32)
    n_pairs = HEAD_DIM // 4
    inv = ROPE_THETA ** (-jnp.arange(n_pairs, dtype=jnp.float32) / n_pairs)
    ang = jnp.concatenate([row[:, None] * inv, col[:, None] * inv], axis=-1)
    return jnp.cos(ang), jnp.sin(ang)


def apply_rope(x, cos, sin):
    b, n, h, dh = x.shape
    xf = x.astype(jnp.float32).reshape(b, n, h, dh // 2, 2)
    x0, x1 = xf[..., 0], xf[..., 1]
    cs, sn = cos[None, :, None, :], sin[None, :, None, :]
    out = jnp.stack([x0 * cs - x1 * sn, x0 * sn + x1 * cs], axis=-1)
    return out.reshape(b, n, h, dh).astype(x.dtype)


def context_attention(q_ctx, k_ctx, v_ctx, sink_logit=None):
    b, n, h, dh = q_ctx.shape
    kv = k_ctx.shape[2]
    g = h // kv
    qg = q_ctx.reshape(b, n, kv, g, dh)
    sc = jnp.einsum('bqkgd,bckd->bkgqc', qg, k_ctx).astype(jnp.float32) * dh ** -0.5
    if sink_logit is not None:
        sink = jnp.broadcast_to(sink_logit.astype(jnp.float32).reshape(1, kv, g, 1, 1), sc.shape[:-1] + (1,))
        p = jax.nn.softmax(jnp.concatenate([sc, sink], axis=-1), axis=-1)[..., :-1]
    else:
        p = jax.nn.softmax(sc, axis=-1)
    o = jnp.einsum('bkgqc,bckd->bqkgd', p.astype(v_ctx.dtype), v_ctx)
    return o.reshape(b, n, h * dh)


def global_gqa(q, k, v, k_ctx, v_ctx):
    b, s, h, dh = q.shape
    kv = k.shape[2]
    g = h // kv
    nb = s // Q_BLOCK
    k_all = jnp.concatenate([k_ctx, k], axis=1)
    v_all = jnp.concatenate([v_ctx, v], axis=1)
    scale = dh ** -0.5
    qb = q.reshape(b, nb, Q_BLOCK, kv, g, dh).swapaxes(0, 1)

    def one_block(q_blk):
        sc = jnp.einsum('bqkgd,bnkd->bkgqn', q_blk, k_all).astype(jnp.float32) * scale
        p = jax.nn.softmax(sc, axis=-1).astype(v_all.dtype)
        return jnp.einsum('bkgqn,bnkd->bqkgd', p, v_all)

    o = lax.map(one_block, qb)
    return o.swapaxes(0, 1).reshape(b, s, h * dh)


def windowed_gqa(q, k, v, k_ctx, v_ctx, sink_logit):
    b, s, h, dh = q.shape
    kv = k.shape[2]
    g = h // kv
    nb = s // Q_BLOCK
    side = WINDOW // Q_BLOCK
    span = (2 * side + 1) * Q_BLOCK
    n_ctx = k_ctx.shape[1]
    pad = ((0, 0), (side * Q_BLOCK, side * Q_BLOCK), (0, 0), (0, 0))
    kp = jnp.pad(k, pad).reshape(b, nb + 2 * side, Q_BLOCK, kv, dh)
    vp = jnp.pad(v, pad).reshape(b, nb + 2 * side, Q_BLOCK, kv, dh)
    kb = jnp.concatenate([kp[:, j:j + nb] for j in range(2 * side + 1)], axis=2)
    vb = jnp.concatenate([vp[:, j:j + nb] for j in range(2 * side + 1)], axis=2)
    qb = q.reshape(b, nb, Q_BLOCK, kv, g, dh)
    a = jnp.arange(Q_BLOCK)[:, None]
    j = jnp.arange(span)[None, :]
    band = jnp.abs(j - side * Q_BLOCK - a) <= WINDOW
    kpos = jnp.arange(nb)[:, None] * Q_BLOCK - side * Q_BLOCK + jnp.arange(span)[None, :]
    valid = band[None] & ((kpos >= 0) & (kpos < s))[:, None, :]
    sink = sink_logit.astype(jnp.float32).reshape(kv, g, 1, 1)
    scale = dh ** -0.5

    def one_block(args):
        q_blk, k_blk, v_blk, m = args
        s_loc = jnp.einsum('bqkgd,bjkd->bkgqj', q_blk, k_blk).astype(jnp.float32) * scale
        s_loc = jnp.where(m, s_loc, NEG_INF)
        s_ctx = jnp.einsum('bqkgd,bckd->bkgqc', q_blk, k_ctx).astype(jnp.float32) * scale
        s_sink = jnp.broadcast_to(sink, s_loc.shape[:-1] + (1,))
        p = jax.nn.softmax(jnp.concatenate([s_loc, s_ctx, s_sink], axis=-1), axis=-1)
        p_loc = p[..., :span].astype(v.dtype)
        p_ctx = p[..., span:span + n_ctx].astype(v.dtype)
        return (jnp.einsum('bkgqj,bjkd->bqkgd', p_loc, v_blk)
                + jnp.einsum('bkgqc,bckd->bqkgd', p_ctx, v_ctx))

    xs = (qb.swapaxes(0, 1), kb.swapaxes(0, 1), vb.swapaxes(0, 1), valid)
    o = lax.map(one_block, xs)
    return o.swapaxes(0, 1).reshape(b, s, h * dh)


def neighbourhood_attention(q, k, v, k_ctx, v_ctx, rpb):
    b, s, h, dh = q.shape
    rows = s // GRID_W
    kr = min(NA_ROWS, rows)
    kw = min(NA_COLS, GRID_W)
    nk = kr * kw
    t = jnp.arange(s, dtype=jnp.int32)
    r = t // GRID_W
    col = t % GRID_W
    r0 = jnp.clip(r - kr // 2, 0, rows - kr)
    c0 = jnp.clip(col - kw // 2, 0, GRID_W - kw)
    key_r = r0[:, None, None] + jnp.arange(kr, dtype=jnp.int32)[None, :, None]
    key_c = c0[:, None, None] + jnp.arange(kw, dtype=jnp.int32)[None, None, :]
    key_idx = (key_r * GRID_W + key_c).reshape(s, nk)
    rel_idx = ((key_r - r[:, None, None] + NA_ROWS - 1) * (2 * NA_COLS - 1)
               + (key_c - col[:, None, None] + NA_COLS - 1)).reshape(s, nk)
    nb = s // Q_BLOCK
    rpb_flat = rpb.reshape(h, -1).astype(jnp.float32)
    scale = dh ** -0.5
    n_ctx = k_ctx.shape[1]

    def one_block(args):
        q_blk, idx, ridx = args
        kg = jnp.take(k, idx.reshape(-1), axis=1).reshape(b, Q_BLOCK, nk, h, dh)
        vg = jnp.take(v, idx.reshape(-1), axis=1).reshape(b, Q_BLOCK, nk, h, dh)
        s_loc = (jnp.einsum('bqhd,bqnhd->bhqn', q_blk, kg).astype(jnp.float32) * scale
                 + jnp.take(rpb_flat, ridx, axis=1)[None])
        s_ctx = jnp.einsum('bqhd,bchd->bhqc', q_blk, k_ctx).astype(jnp.float32) * scale
        p = jax.nn.softmax(jnp.concatenate([s_loc, s_ctx], axis=-1), axis=-1)
        p_loc = p[..., :nk].astype(v.dtype)
        p_ctx = p[..., nk:nk + n_ctx].astype(v.dtype)
        return (jnp.einsum('bhqn,bqnhd->bqhd', p_loc, vg)
                + jnp.einsum('bhqc,bchd->bqhd', p_ctx, v_ctx))

    xs = (q.reshape(b, nb, Q_BLOCK, h, dh).swapaxes(0, 1),
          key_idx.reshape(nb, Q_BLOCK, nk), rel_idx.reshape(nb, Q_BLOCK, nk))
    o = lax.map(one_block, xs)
    return o.swapaxes(0, 1).reshape(b, s, h * dh)


def hyena_positional_features(n):
    t = jnp.linspace(0.0, 1.0, n, dtype=jnp.float32)[:, None]
    bands = (HYENA_EMB - 1) // 2
    w = 2.0 * math.pi * jnp.arange(n, dtype=jnp.float32)[:, None] / n
    f = jnp.linspace(1e-4, bands - 1, bands, dtype=jnp.float32)[None, :]
    return jnp.concatenate([t, jnp.cos(f * w), -jnp.sin(f * w)], axis=-1)


def hyena_filters(n, w1, b1, w2, b2, freq, w3):
    f32 = jnp.float32
    z = hyena_positional_features(n)
    fr = freq.astype(f32)
    hdn = jnp.sin(fr * (z @ w1.astype(f32) + b1.astype(f32)))
    hdn = jnp.sin(fr * (hdn @ w2.astype(f32) + b2.astype(f32)))
    filt = hdn @ w3.astype(f32)
    t = jnp.linspace(0.0, 1.0, n, dtype=f32)[:, None]
    deltas = jnp.linspace(math.log(HYENA_DECAY_TARGET) / HYENA_LONG_DECAY_PCT,
                          math.log(HYENA_DECAY_TARGET) / HYENA_SHORT_DECAY_PCT, HYENA_WIDTH, dtype=f32)
    decay = jnp.exp(-t * jnp.abs(deltas))
    return filt.reshape(n, HYENA_ORDER, 2, HYENA_WIDTH) * decay[:, None, None, :]


def bidir_long_conv(z, h_fwd, h_bwd, d_skip):
    n = z.shape[1]
    taps = jnp.concatenate([h_fwd, jnp.zeros_like(h_fwd[:1]), h_bwd[:0:-1]], axis=0)
    taps = taps * lax.rsqrt(jnp.sum(taps * taps, axis=0, keepdims=True) + EPS)
    zf = jnp.fft.rfft(z.astype(jnp.float32), n=2 * n, axis=1)
    tf = jnp.fft.rfft(taps, n=2 * n, axis=0)
    y = jnp.fft.irfft(zf * tf[None], n=2 * n, axis=1)[:, :n]
    return (y + z.astype(jnp.float32) * d_skip.astype(jnp.float32)).astype(z.dtype)


def centred_depthwise_conv(u, w, bias):
    pad = w.shape[0] // 2
    y = lax.conv_general_dilated(u, w[:, None, :].astype(u.dtype), window_strides=(1,),
                                 padding=[(pad, pad)], dimension_numbers=('NWC', 'WIO', 'NWC'),
                                 feature_group_count=u.shape[-1])
    return y + bias


def hyena_branch(u, conv_w, conv_b, w1, b1, w2, b2, freq, w3, d_skip):
    n = u.shape[1]
    filt = hyena_filters(n, w1, b1, w2, b2, freq, w3)
    uc = centred_depthwise_conv(u, conv_w, conv_b)
    parts = jnp.split(uc, HYENA_ORDER + 1, axis=-1)
    z = parts[0]
    for o in range(HYENA_ORDER):
        z = parts[o + 1] * bidir_long_conv(z, filt[:, o, 0], filt[:, o, 1], d_skip[o])
    return z


def even_mixer(p, p_ctx, q_norm, k_norm, conv_w, conv_b, w1, b1, w2, b2, freq, w3, d_skip, cos, sin, need_ctx):
    aq, ak, av, hu, ag, hg = split_cols(p, EVEN_SPLITS)
    caq, cak, cav, chu, cag, chg = split_cols(p_ctx, EVEN_SPLITS)
    q = apply_rope(rms_norm(heads(aq, N_Q_HEADS), q_norm), cos, sin)
    k = apply_rope(rms_norm(heads(ak, N_KV_HEADS), k_norm), cos, sin)
    v = heads(av, N_KV_HEADS)
    k_c = rms_norm(heads(cak, N_KV_HEADS), k_norm)
    v_c = heads(cav, N_KV_HEADS)
    y_a = global_gqa(q, k, v, k_c, v_c)
    y_b = hyena_branch(hu, conv_w, conv_b, w1, b1, w2, b2, freq, w3, d_skip)
    y = jnp.concatenate([y_a * jax.nn.silu(ag), y_b * jax.nn.silu(hg)], axis=-1)
    if not need_ctx:
        return y, None
    q_c = rms_norm(heads(caq, N_Q_HEADS), q_norm)
    y_ca = context_attention(q_c, k_c, v_c)
    y_cb = hyena_branch(chu, conv_w, conv_b, w1, b1, w2, b2, freq, w3, d_skip)
    y_c = jnp.concatenate([y_ca * jax.nn.silu(cag), y_cb * jax.nn.silu(chg)], axis=-1)
    return y, y_c


def odd_mixer(p, p_ctx, wq_norm, wk_norm, sink, nq_norm, nk_norm, rpb, cos, sin, need_ctx):
    wq, wk, wv, nq, nk_, nv, wg, ng = split_cols(p, ODD_SPLITS)
    cwq, cwk, cwv, cnq, cnk, cnv, cwg, cng = split_cols(p_ctx, ODD_SPLITS)
    q_w = apply_rope(rms_norm(heads(wq, N_Q_HEADS), wq_norm), cos, sin)
    k_w = apply_rope(rms_norm(heads(wk, N_KV_HEADS), wk_norm), cos, sin)
    v_w = heads(wv, N_KV_HEADS)
    kc_w = rms_norm(heads(cwk, N_KV_HEADS), wk_norm)
    vc_w = heads(cwv, N_KV_HEADS)
    y_w = windowed_gqa(q_w, k_w, v_w, kc_w, vc_w, sink)
    q_n = rms_norm(heads(nq, N_Q_HEADS), nq_norm)
    k_n = rms_norm(heads(nk_, N_Q_HEADS), nk_norm)
    v_n = heads(nv, N_Q_HEADS)
    kc_n = rms_norm(heads(cnk, N_Q_HEADS), nk_norm)
    vc_n = heads(cnv, N_Q_HEADS)
    y_n = neighbourhood_attention(q_n, k_n, v_n, kc_n, vc_n, rpb)
    y = jnp.concatenate([y_w * jax.nn.silu(wg), y_n * jax.nn.silu(ng)], axis=-1)
    if not need_ctx:
        return y, None
    y_cw = context_attention(rms_norm(heads(cwq, N_Q_HEADS), wq_norm), kc_w, vc_w, sink)
    y_cn = context_attention(rms_norm(heads(cnq, N_Q_HEADS), nq_norm), kc_n, vc_n)
    y_c = jnp.concatenate([y_cw * jax.nn.silu(cwg), y_cn * jax.nn.silu(cng)], axis=-1)
    return y, y_c


def setup_inputs(seed: int = 0) -> dict:
    key = jax.random.key(seed)
    keys = iter(jax.random.split(key, 32))

    def nrm(shape, std):
        return std * jax.random.normal(next(keys), shape, jnp.float32)

    n_filter_out = HYENA_ORDER * 2 * HYENA_WIDTH
    n_hy_in = (HYENA_ORDER + 1) * HYENA_WIDTH
    return {
        'x': nrm((BATCH, SEQ, D_MODEL), 1.0),
        'c': nrm((BATCH, D_MODEL), 1.0),
        'ctx': nrm((BATCH, CTX_LEN, D_MODEL), 1.0),
        'c_ctx': nrm((D_MODEL,), 1.0),
        'norm_g': 1.0 + nrm((DEPTH, D_MODEL), 0.1),
        'w_ada': nrm((DEPTH, D_MODEL, 3 * D_MODEL), D_MODEL ** -0.5),
        'b_ada': nrm((DEPTH, 3 * D_MODEL), 0.01),
        'w_in': nrm((DEPTH, D_MODEL, IN_WIDTH), D_MODEL ** -0.5),
        'w_out': nrm((DEPTH, MIX_WIDTH, D_MODEL), MIX_WIDTH ** -0.5),
        'glob_q_norm': 1.0 + nrm((N_EVEN, HEAD_DIM), 0.1),
        'glob_k_norm': 1.0 + nrm((N_EVEN, HEAD_DIM), 0.1),
        'hy_conv_w': nrm((N_EVEN, HYENA_SHORT_CONV, n_hy_in), HYENA_SHORT_CONV ** -0.5),
        'hy_conv_b': nrm((N_EVEN, n_hy_in), 0.01),
        'hy_w1': nrm((N_EVEN, HYENA_EMB, HYENA_HIDDEN), HYENA_EMB ** -0.5),
        'hy_b1': nrm((N_EVEN, HYENA_HIDDEN), 0.01),
        'hy_w2': nrm((N_EVEN, HYENA_HIDDEN, HYENA_HIDDEN), HYENA_HIDDEN ** -0.5),
        'hy_b2': nrm((N_EVEN, HYENA_HIDDEN), 0.01),
        'hy_freq': 1.0 + nrm((N_EVEN, HYENA_HIDDEN), 0.1),
        'hy_w3': nrm((N_EVEN, HYENA_HIDDEN, n_filter_out), HYENA_HIDDEN ** -0.5),
        'hy_skip': nrm((N_EVEN, HYENA_ORDER, HYENA_WIDTH), 0.5),
        'win_q_norm': 1.0 + nrm((N_ODD, HEAD_DIM), 0.1),
        'win_k_norm': 1.0 + nrm((N_ODD, HEAD_DIM), 0.1),
        'win_sink': nrm((N_ODD, N_Q_HEADS), 0.5),
        'nat_q_norm': 1.0 + nrm((N_ODD, HEAD_DIM), 0.1),
        'nat_k_norm': 1.0 + nrm((N_ODD, HEAD_DIM), 0.1),
        'nat_rpb': nrm((N_ODD, N_Q_HEADS, 2 * NA_ROWS - 1, 2 * NA_COLS - 1), 0.1),
    }


def reference(x, c, ctx, c_ctx, norm_g, w_ada, b_ada, w_in, w_out,
              glob_q_norm, glob_k_norm, hy_conv_w, hy_conv_b, hy_w1, hy_b1, hy_w2, hy_b2,
              hy_freq, hy_w3, hy_skip, win_q_norm, win_k_norm, win_sink,
              nat_q_norm, nat_k_norm, nat_rpb):
    cos, sin = axial_rope_tables(x.shape[1])
    x_ctx = ctx
    for layer in range(DEPTH):
        need_ctx = layer < DEPTH - 1
        shift, scale, gate = adaln(c, w_ada[layer], b_ada[layer])
        c_shift, c_scale, c_gate = adaln(c_ctx, w_ada[layer], b_ada[layer])
        h = modulated_norm(x, norm_g[layer], shift[:, None], scale[:, None])
        h_ctx = modulated_norm(x_ctx, norm_g[layer], c_shift, c_scale)
        p = jnp.matmul(h, w_in[layer])
        p_ctx = jnp.matmul(h_ctx, w_in[layer])
        i = layer // 2
        if layer % 2 == 0:
            y, y_ctx = even_mixer(p, p_ctx, glob_q_norm[i], glob_k_norm[i], hy_conv_w[i], hy_conv_b[i],
                                  hy_w1[i], hy_b1[i], hy_w2[i], hy_b2[i], hy_freq[i], hy_w3[i], hy_skip[i],
                                  cos, sin, need_ctx)
        else:
            y, y_ctx = odd_mixer(p, p_ctx, win_q_norm[i], win_k_norm[i], win_sink[i],
                                 nat_q_norm[i], nat_k_norm[i], nat_rpb[i], cos, sin, need_ctx)
        x = x + gate[:, None] * jnp.matmul(y, w_out[layer])
        if need_ctx:
            x_ctx = x_ctx + c_gate * jnp.matmul(y_ctx, w_out[layer])
    return x
```

```python
import functools
import math

import numpy as np
import jax
import jax.numpy as jnp
from jax import lax
from jax.experimental import pallas as pl
from jax.experimental.pallas import tpu as pltpu

F32 = jnp.float32
BF16 = jnp.bfloat16
HIGHEST = lax.Precision.HIGHEST

GRID_W = 64
HEAD_DIM = 64
GROUP_WIDTH = 512
N_Q_HEADS = 8
N_KV_HEADS = 2
GQA_GROUP = N_Q_HEADS // N_KV_HEADS
HYENA_WIDTH = 512
HYENA_ORDER = 2
HYENA_EMB = 33
HYENA_HIDDEN = 64
HYENA_DECAY_TARGET = 1e-2
HYENA_SHORT_DECAY_PCT = 0.3
HYENA_LONG_DECAY_PCT = 1.5
WINDOW = 128
NA_ROWS = 8
NA_COLS = 16
ROPE_THETA = 10000.0
EPS = 1e-6
NEG_INF = -1e30
QK_SCALE = HEAD_DIM ** -0.5

LANES = 128
SUB = 8
HALO = 16
FFT_N1 = 128
VMEM_LIMIT = 56 * 1024 * 1024


def _dot(a, b, precision=None):
    return jnp.dot(a, b, preferred_element_type=F32, precision=precision)


def _dot_nt(a, b):
    return lax.dot_general(a, b, (((1,), (1,)), ((), ())), preferred_element_type=F32)


def _params(sem):
    return pltpu.CompilerParams(dimension_semantics=sem, vmem_limit_bytes=VMEM_LIMIT)


def _silu(v):
    return v / (1.0 + jnp.exp(-v))


def _ada_kernel(cond_ref, w_ref, b_ref, o_ref):
    o_ref[0] = _dot(_silu(cond_ref[...]), w_ref[0], HIGHEST) + b_ref[0]


def _ada_call(cond, w_ada, b_ada):
    depth, d, d3 = w_ada.shape
    tn = 1024
    return pl.pallas_call(
        _ada_kernel,
        grid=(depth, d3 // tn),
        in_specs=[pl.BlockSpec((8, d), lambda l, j: (0, 0)),
                  pl.BlockSpec((1, d, tn), lambda l, j: (l, 0, j)),
                  pl.BlockSpec((1, 1, tn), lambda l, j: (l, 0, j))],
        out_specs=pl.BlockSpec((1, 8, tn), lambda l, j: (l, 0, j)),
        out_shape=jax.ShapeDtypeStruct((depth, 8, d3), F32),
        compiler_params=_params(("parallel", "parallel")),
        name="adaln",
    )(cond, w_ada, b_ada.reshape(depth, 1, d3))


def _inproj_kernel(*refs, plan, tm, tps, has_halo, has_rope):
    it = iter(refs)
    x_ref = next(it)
    if has_halo:
        xp_ref, xn_ref = next(it), next(it)
    sh_ref, sc_ref, g_ref, w_ref = next(it), next(it), next(it), next(it)
    if has_rope:
        cos_ref, s1_ref, s2_ref = next(it), next(it), next(it)
    gains_ref, mavg_ref = next(it), next(it)
    if has_halo:
        cw_ref, cb_ref = next(it), next(it)
    outs = [next(it) for _ in plan]
    if has_halo:
        xe_ref, hu_ref = next(it), next(it)

    g, sh, sc = g_ref[...], sh_ref[...], sc_ref[...]

    def modnorm(xv):
        ms = jnp.mean(xv * xv, axis=-1, keepdims=True)
        return (xv * lax.rsqrt(ms + EPS) * g) * (1.0 + sc) + sh

    h = modnorm(x_ref[...]).astype(BF16)
    mavg = mavg_ref[...]

    for seg, o_ref in zip(plan, outs):
        kind, start, width = seg[0], seg[1], seg[2]
        if kind == "hy":
            xe_ref[0:HALO] = modnorm(xp_ref[...]).astype(BF16)
            xe_ref[HALO:HALO + tm] = h
            xe_ref[HALO + tm:HALO + tm + HALO] = modnorm(xn_ref[...]).astype(BF16)
            hu_ref[...] = _dot(xe_ref[...], w_ref[:, start:start + width])
            pos = pl.program_id(0) % tps
            row = lax.broadcasted_iota(jnp.int32, (tm, 1), 0)
            kill_prev = jnp.where(pos == 0, 0, -1)
            kill_next = jnp.where(pos == tps - 1, tm - 1, -1)
            a = jnp.where(row == kill_prev, 0.0, hu_ref[pl.ds(HALO - 1, tm), :])
            b = hu_ref[pl.ds(HALO, tm), :]
            c = jnp.where(row == kill_next, 0.0, hu_ref[pl.ds(HALO + 1, tm), :])
            uc = a * cw_ref[0:1, :] + b * cw_ref[1:2, :] + c * cw_ref[2:3, :] + cb_ref[...]
            o_ref[...] = uc.reshape(o_ref.shape)
            continue
        pv = _dot(h, w_ref[:, start:start + width])
        if kind == "gate":
            o_ref[...] = _silu(pv).astype(o_ref.dtype)
        elif kind == "v":
            for hd in range(width // HEAD_DIM):
                o_ref[hd] = pv[:, hd * HEAD_DIM:(hd + 1) * HEAD_DIM].astype(o_ref.dtype)
        else:
            gain_row, rope, scale = seg[3], seg[4], seg[5]
            gain = gains_ref[gain_row:gain_row + 1, :]
            for c in range(width // LANES):
                v = pv[:, c * LANES:(c + 1) * LANES]
                sq = v * v
                hi = sq.astype(BF16)
                lo = (sq - hi.astype(F32)).astype(BF16)
                ms = _dot(hi, mavg) + _dot(lo, mavg)
                vn = v * lax.rsqrt(ms + EPS) * gain
                if rope:
                    vn = (vn * cos_ref[...] + pltpu.roll(vn, LANES - 1, 1) * s1_ref[...]
                          + pltpu.roll(vn, 1, 1) * s2_ref[...])
                vn = (vn * scale).astype(o_ref.dtype)
                o_ref[2 * c] = vn[:, :HEAD_DIM]
                o_ref[2 * c + 1] = vn[:, HEAD_DIM:]


def _inproj_call(x2d, shift, scale, norm_g, w_bf, plan, *, seq_len, tm, per_batch_mod,
                 rope_tabs=None, gains=None, conv=None, hy_t_layout=False, name="inproj"):
    r, d = x2d.shape
    tps = seq_len // tm
    n_tiles = r // tm
    n_seq = r // seq_len
    has_halo = any(s[0] == "hy" for s in plan)
    has_rope = rope_tabs is not None
    hb = tm // HALO
    mod_map = (lambda i: (i // tps, 0, 0)) if per_batch_mod else (lambda i: (0, 0, 0))

    args, specs = [x2d], [pl.BlockSpec((tm, d), lambda i: (i, 0))]
    if has_halo:
        args += [x2d, x2d]
        specs += [pl.BlockSpec((HALO, d), lambda i: (jnp.maximum(i * hb - 1, 0), 0)),
                  pl.BlockSpec((HALO, d), lambda i: (jnp.minimum((i + 1) * hb, r // HALO - 1), 0))]
    args += [shift, scale, norm_g.reshape(1, d), w_bf]
    specs += [pl.BlockSpec((None, 1, d), mod_map), pl.BlockSpec((None, 1, d), mod_map),
              pl.BlockSpec((1, d), lambda i: (0, 0)),
              pl.BlockSpec(w_bf.shape, lambda i: (0, 0))]
    if has_rope:
        args += list(rope_tabs)
        specs += [pl.BlockSpec((tm, LANES), lambda i: (i % tps, 0))] * 3
    mavg = np.kron(np.eye(LANES // HEAD_DIM), np.full((HEAD_DIM, HEAD_DIM), 1.0 / HEAD_DIM))
    args += [gains, jnp.asarray(mavg, BF16)]
    specs += [pl.BlockSpec(gains.shape, lambda i: (0, 0)), pl.BlockSpec((LANES, LANES), lambda i: (0, 0))]
    if has_halo:
        cw, cb = conv
        args += [cw, cb.reshape(1, -1)]
        specs += [pl.BlockSpec(cw.shape, lambda i: (0, 0)), pl.BlockSpec((1, cb.shape[-1]), lambda i: (0, 0))]

    out_shapes, out_specs = [], []
    for seg in plan:
        kind, width = seg[0], seg[2]
        if kind == "hy":
            if hy_t_layout:
                out_shapes.append(jax.ShapeDtypeStruct((n_seq, tm // SUB, tps * SUB, width), F32))
                out_specs.append(pl.BlockSpec((None, tm // SUB, SUB, width), lambda i: (i // tps, 0, i % tps, 0)))
            else:
                out_shapes.append(jax.ShapeDtypeStruct((r, width), F32))
                out_specs.append(pl.BlockSpec((tm, width), lambda i: (i, 0)))
        elif kind == "gate":
            out_shapes.append(jax.ShapeDtypeStruct((r, width), BF16))
            out_specs.append(pl.BlockSpec((tm, width), lambda i: (i, 0)))
        else:
            nh = width // HEAD_DIM
            out_shapes.append(jax.ShapeDtypeStruct((nh, r, HEAD_DIM), BF16))
            out_specs.append(pl.BlockSpec((nh, tm, HEAD_DIM), lambda i: (0, i, 0)))
    scratch = []
    if has_halo:
        hw = [s[2] for s in plan if s[0] == "hy"][0]
        scratch = [pltpu.VMEM((tm + 2 * HALO, d), BF16), pltpu.VMEM((tm + 2 * HALO, hw), F32)]
    kern = functools.partial(_inproj_kernel, plan=plan, tm=tm, tps=tps, has_halo=has_halo, has_rope=has_rope)
    return pl.pallas_call(
        kern, grid=(n_tiles,), in_specs=specs, out_specs=out_specs, out_shape=out_shapes,
        scratch_shapes=scratch, compiler_params=_params(("parallel",)), name=name,
    )(*args)


def _flash_kernel(q_ref, kc_ref, vc_ref, *rest, tq, tk, n_lat_chunks):
    if n_lat_chunks:
        kl_ref, vl_ref, o_ref, m_ref, l_ref, acc_ref = rest
    else:
        o_ref, m_ref, l_ref, acc_ref = rest
    rows = GQA_GROUP * tq
    q = q_ref[...].reshape(rows, HEAD_DIM)
    m_ref[...] = jnp.full((rows, 1), NEG_INF, F32)
    l_ref[...] = jnp.zeros((rows, 1), F32)
    acc_ref[...] = jnp.zeros((rows, HEAD_DIM), F32)

    def step(kb, vb):
        s = _dot_nt(q, kb)
        m_prev = m_ref[...]
        m_new = jnp.maximum(m_prev, jnp.max(s, axis=-1, keepdims=True))
        alpha = jnp.exp(m_prev - m_new)
        p = jnp.exp(s - m_new)
        l_ref[...] = alpha * l_ref[...] + jnp.sum(p, axis=-1, keepdims=True)
        acc_ref[...] = alpha * acc_ref[...] + _dot(p.astype(BF16), vb)
        m_ref[...] = m_new

    step(kc_ref[...], vc_ref[...])
    if n_lat_chunks:
        def body(c, carry):
            off = pl.multiple_of(c * tk, tk)
            step(kl_ref[pl.ds(off, tk), :], vl_ref[pl.ds(off, tk), :])
            return carry
        lax.fori_loop(0, n_lat_chunks, body, 0)
    o = acc_ref[...] / l_ref[...]
    o_ref[...] = jnp.concatenate([o[gi * tq:(gi + 1) * tq] for gi in range(GQA_GROUP)], axis=-1).astype(o_ref.dtype)


def _flash_call(q, kc, vc, kl=None, vl=None, *, tq, tk=512, name="gqa"):
    _, b, sq, _ = q.shape
    c = kc.shape[2]
    args = [q, kc, vc]
    specs = [pl.BlockSpec((GQA_GROUP, None, tq, HEAD_DIM), lambda bi, j, i: (j, bi, i, 0)),
             pl.BlockSpec((None, None, c, HEAD_DIM), lambda bi, j, i: (j, bi, 0, 0)),
             pl.BlockSpec((None, None, c, HEAD_DIM), lambda bi, j, i: (j, bi, 0, 0))]
    n_lat = 0
    if kl is not None:
        s = kl.shape[2]
        n_lat = s // tk
        args += [kl, vl]
        specs += [pl.BlockSpec((None, None, s, HEAD_DIM), lambda bi, j, i: (j, bi, 0, 0))] * 2
    rows = GQA_GROUP * tq
    kern = functools.partial(_flash_kernel, tq=tq, tk=tk, n_lat_chunks=n_lat)
    return pl.pallas_call(
        kern, grid=(b, N_KV_HEADS, sq // tq), in_specs=specs,
        out_specs=pl.BlockSpec((None, tq, GQA_GROUP * HEAD_DIM), lambda bi, j, i: (bi, i, j)),
        out_shape=jax.ShapeDtypeStruct((b, sq, GROUP_WIDTH), BF16),
        scratch_shapes=[pltpu.VMEM((rows, 1), F32), pltpu.VMEM((rows, 1), F32), pltpu.VMEM((rows, HEAD_DIM), F32)],
        compiler_params=_params(("parallel", "parallel", "arbitrary")), name=name,
    )(*args)


def _window_kernel(sink_ref, q_ref, kc_ref, vc_ref, kl_ref, vl_ref, o_ref, *, tq, span, seq):
    j, i = pl.program_id(1), pl.program_id(2)
    start = pl.multiple_of(jnp.clip(i * tq - WINDOW, 0, seq - span), LANES)
    kb, vb = kl_ref[pl.ds(start, span), :], vl_ref[pl.ds(start, span), :]
    kc, vc = kc_ref[...], vc_ref[...]
    qpos = i * tq + lax.broadcasted_iota(jnp.int32, (tq, span), 0)
    kpos = start + lax.broadcasted_iota(jnp.int32, (tq, span), 1)
    valid = jnp.abs(qpos - kpos) <= WINDOW
    outs = []
    for gi in range(GQA_GROUP):
        q = q_ref[gi]
        s_loc = jnp.where(valid, _dot_nt(q, kb), NEG_INF)
        s_ctx = _dot_nt(q, kc)
        sk = sink_ref[j * GQA_GROUP + gi]
        m = jnp.maximum(jnp.maximum(jnp.max(s_loc, axis=-1, keepdims=True),
                                    jnp.max(s_ctx, axis=-1, keepdims=True)), sk)
        p_loc, p_ctx = jnp.exp(s_loc - m), jnp.exp(s_ctx - m)
        l = (jnp.sum(p_loc, axis=-1, keepdims=True) + jnp.sum(p_ctx, axis=-1, keepdims=True)
             + jnp.exp(sk - m))
        outs.append((_dot(p_loc.astype(BF16), vb) + _dot(p_ctx.astype(BF16), vc)) / l)
    o_ref[...] = jnp.concatenate(outs, axis=-1).astype(o_ref.dtype)


def _window_call(sink, q, kc, vc, kl, vl, *, tq=256):
    _, b, s, _ = q.shape
    c = kc.shape[2]
    span = tq + 2 * WINDOW
    kern = functools.partial(_window_kernel, tq=tq, span=span, seq=s)
    full = lambda n: pl.BlockSpec((None, None, n, HEAD_DIM), lambda bi, j, i: (j, bi, 0, 0))
    return pl.pallas_call(
        kern, grid=(b, N_KV_HEADS, s // tq),
        in_specs=[pl.BlockSpec(memory_space=pltpu.SMEM),
                  pl.BlockSpec((GQA_GROUP, None, tq, HEAD_DIM), lambda bi, j, i: (j, bi, i, 0)),
                  full(c), full(c), full(s), full(s)],
        out_specs=pl.BlockSpec((None, tq, GQA_GROUP * HEAD_DIM), lambda bi, j, i: (bi, i, j)),
        out_shape=jax.ShapeDtypeStruct((b, s, GROUP_WIDTH), BF16),
        compiler_params=_params(("parallel", "parallel", "arbitrary")), name="window_gqa",
    )(sink, q, kc, vc, kl, vl)


NA_BLOCK = 2 * GRID_W
NA_KEY_BLOCKS = 5
NA_SHIFTS = 9


def _rpb_table_kernel(rpb_ref, o_ref):
    h, ei = pl.program_id(0), pl.program_id(1)
    e = 2 * ei - (NA_SHIFTS - 1)
    sub = lax.broadcasted_iota(jnp.int32, (NA_BLOCK, NA_BLOCK), 0)
    lane = lax.broadcasted_iota(jnp.int32, (NA_BLOCK, NA_BLOCK), 1)
    qc, kc = sub % GRID_W, lane % GRID_W
    dr = e + lane // GRID_W - sub // GRID_W
    dc = kc - qc
    c0 = jnp.clip(qc - NA_COLS // 2, 0, GRID_W - NA_COLS)
    n_dc = 2 * NA_COLS - 1
    per_head = (2 * NA_ROWS - 1) * n_dc
    val = jnp.zeros((NA_BLOCK, NA_BLOCK), F32)
    for ddr in (-1, 0, 1):
        d = e + ddr
        base = h * per_head + (jnp.clip(d, 1 - NA_ROWS, NA_ROWS - 1) + NA_ROWS - 1) * n_dc

        def body(c, v, d=d, base=base):
            return jnp.where(jnp.where(dr == d, dc, n_dc) == c - (NA_COLS - 1), rpb_ref[base + c], v)
        val = lax.fori_loop(0, n_dc, body, val)
    ok = jnp.where(kc >= c0, jnp.where(kc < c0 + NA_COLS, jnp.abs(dr), NA_ROWS), NA_ROWS) < NA_ROWS
    o_ref[0, 0] = jnp.where(ok, val, NEG_INF)


def _rpb_table_call(rpb):
    h = rpb.shape[0]
    return pl.pallas_call(
        _rpb_table_kernel, grid=(h, NA_SHIFTS),
        in_specs=[pl.BlockSpec(memory_space=pltpu.SMEM)],
        out_specs=pl.BlockSpec((1, 1, NA_BLOCK, NA_BLOCK), lambda hh, e: (hh, e, 0, 0)),
        out_shape=jax.ShapeDtypeStruct((h, NA_SHIFTS, NA_BLOCK, NA_BLOCK), F32),
        compiler_params=_params(("parallel", "parallel")), name="rpb_table",
    )(rpb.reshape(-1))


def _na_first_key_row(i, rows):
    return jnp.clip(2 * i - NA_ROWS // 2, 0, rows - 2 * NA_KEY_BLOCKS)


def _na_kernel(*refs, rows):
    q_ref = refs[0]
    k_refs = refs[1:1 + NA_KEY_BLOCKS]
    v_refs = refs[1 + NA_KEY_BLOCKS:1 + 2 * NA_KEY_BLOCKS]
    kc_ref, vc_ref, tab_ref, o_ref = refs[1 + 2 * NA_KEY_BLOCKS:]
    i = pl.program_id(1)
    rs = _na_first_key_row(i, rows)
    sub = lax.broadcasted_iota(jnp.int32, (NA_BLOCK, NA_BLOCK), 0)
    lane = lax.broadcasted_iota(jnp.int32, (NA_BLOCK, NA_BLOCK), 1)
    rq = 2 * i + sub // GRID_W
    r0 = jnp.clip(rq - NA_ROWS // 2, 0, rows - NA_ROWS)
    row_masks, shift_idx = [], []
    for t in range(NA_KEY_BLOCKS):
        off = rs + 2 * t + lane // GRID_W - r0
        row_masks.append(jnp.where(jnp.where(off >= 0, off, NA_ROWS) < NA_ROWS, 0.0, NEG_INF))
        shift_idx.append((rs - 2 * i + 2 * t + NA_SHIFTS - 1) // 2)
    outs = []
    for h in range(N_Q_HEADS):
        q = q_ref[h]
        s_loc = [_dot_nt(q, k_refs[t][h]) + tab_ref[h, shift_idx[t]] + row_masks[t]
                 for t in range(NA_KEY_BLOCKS)]
        s_ctx = _dot_nt(q, kc_ref[h])
        m = jnp.max(s_ctx, axis=-1, keepdims=True)
        for s in s_loc:
            m = jnp.maximum(m, jnp.max(s, axis=-1, keepdims=True))
        p_ctx = jnp.exp(s_ctx - m)
        l = jnp.sum(p_ctx, axis=-1, keepdims=True)
        o = _dot(p_ctx.astype(BF16), vc_ref[h])
        for t in range(NA_KEY_BLOCKS):
            p = jnp.exp(s_loc[t] - m)
            l = l + jnp.sum(p, axis=-1, keepdims=True)
            o = o + _dot(p.astype(BF16), v_refs[t][h])
        outs.append(o / l)
    o_ref[...] = jnp.concatenate(outs, axis=-1).astype(o_ref.dtype)


def _na_call(q, k, v, kc, vc, table):
    nh, b, s, _ = q.shape
    c = kc.shape[2]
    rows = s // GRID_W
    blk = lambda t: pl.BlockSpec(
        (nh, None, NA_BLOCK, HEAD_DIM), lambda bi, i, t=t: (0, bi, _na_first_key_row(i, rows) // 2 + t, 0))
    ctx = pl.BlockSpec((nh, None, c, HEAD_DIM), lambda bi, i: (0, bi, 0, 0))
    return pl.pallas_call(
        functools.partial(_na_kernel, rows=rows), grid=(b, s // NA_BLOCK),
        in_specs=([pl.BlockSpec((nh, None, NA_BLOCK, HEAD_DIM), lambda bi, i: (0, bi, i, 0))]
                  + [blk(t) for t in range(NA_KEY_BLOCKS)] * 2
                  + [ctx, ctx, pl.BlockSpec(table.shape, lambda bi, i: (0, 0, 0, 0))]),
        out_specs=pl.BlockSpec((None, NA_BLOCK, nh * HEAD_DIM), lambda bi, i: (bi, i, 0)),
        out_shape=jax.ShapeDtypeStruct((b, s, nh * HEAD_DIM), BF16),
        compiler_params=_params(("parallel", "arbitrary")), name="nbr_attn",
    )(q, *([k] * NA_KEY_BLOCKS), *([v] * NA_KEY_BLOCKS), kc, vc, table)


def _hyena_features(idx, n):
    bands = (HYENA_EMB - 1) // 2
    idx_f = idx.astype(F32)
    t = idx_f / (n - 1)
    w = 2.0 * math.pi * idx_f / n
    f = jnp.linspace(1e-4, bands - 1, bands, dtype=F32)[None, :]
    fw = f * w[:, None]
    z = jnp.concatenate([t[:, None], jnp.cos(fw), -jnp.sin(fw)], axis=-1)
    return jnp.pad(z, ((0, 0), (0, HYENA_HIDDEN - HYENA_EMB)))


def _filter_kernel(z_ref, w1_ref, b1_ref, w2_ref, b2_ref, fr_ref, w3_ref, dl_ref, taps_ref, ss_ref, *, tm, zero_tile):
    i = pl.program_id(0)
    zf = z_ref[...]
    fr = fr_ref[...]
    h1 = jnp.sin(fr * (_dot(zf, w1_ref[...], HIGHEST) + b1_ref[...]))
    h2 = jnp.sin(fr * (_dot(h1, w2_ref[...], HIGHEST) + b2_ref[...]))
    filt = _dot(h2, w3_ref[...], HIGHEST)
    decay = jnp.exp(-zf[:, 0:1] * dl_ref[...])
    row = lax.broadcasted_iota(jnp.int32, (tm, 1), 0)
    kill = jnp.where(i == zero_tile, 0, -1)

    @pl.when(i == 0)
    def _():
        ss_ref[...] = jnp.zeros_like(ss_ref)

    for o in range(HYENA_ORDER):
        tp = jnp.where(row == kill, 0.0, filt[:, o * HYENA_WIDTH:(o + 1) * HYENA_WIDTH] * decay)
        taps_ref[o] = tp.reshape(taps_ref.shape[1:])
        ss_ref[o] += jnp.sum(tp * tp, axis=0, keepdims=True)


def _filter_call(n, w1, b1, w2, b2, freq, w3, *, tm, t_layout):
    n_tiles = 2 * n // tm
    pos = jnp.arange(2 * n, dtype=jnp.int32)
    idx = jnp.where(pos < n, pos, 2 * n - pos)
    z = _hyena_features(idx, n)
    w1p = jnp.pad(w1, ((0, HYENA_HIDDEN - HYENA_EMB), (0, 0)))
    w3d = w3.reshape(HYENA_HIDDEN, HYENA_ORDER, 2, HYENA_WIDTH).transpose(2, 0, 1, 3).reshape(
        2, HYENA_HIDDEN, HYENA_ORDER * HYENA_WIDTH)
    deltas = np.abs(np.linspace(math.log(HYENA_DECAY_TARGET) / HYENA_LONG_DECAY_PCT,
                                math.log(HYENA_DECAY_TARGET) / HYENA_SHORT_DECAY_PCT, HYENA_WIDTH))
    half = n_tiles // 2
    small = lambda a: pl.BlockSpec(a.shape, lambda i: (0,) * a.ndim)
    b1r, b2r, frr = b1.reshape(1, -1), b2.reshape(1, -1), freq.reshape(1, -1)
    dl = jnp.asarray(deltas.reshape(1, -1), F32)
    if t_layout:
        taps_shape = (HYENA_ORDER, tm // SUB, n_tiles * SUB, HYENA_WIDTH)
        taps_spec = pl.BlockSpec((HYENA_ORDER, tm // SUB, SUB, HYENA_WIDTH), lambda i: (0, 0, i, 0))
    else:
        taps_shape = (HYENA_ORDER, 2 * n, HYENA_WIDTH)
        taps_spec = pl.BlockSpec((HYENA_ORDER, tm, HYENA_WIDTH), lambda i: (0, i, 0))
    return pl.pallas_call(
        functools.partial(_filter_kernel, tm=tm, zero_tile=half), grid=(n_tiles,),
        in_specs=[pl.BlockSpec((tm, HYENA_HIDDEN), lambda i: (i, 0)), small(w1p), small(b1r), small(w2), small(b2r),
                  small(frr), pl.BlockSpec((None, HYENA_HIDDEN, HYENA_ORDER * HYENA_WIDTH), lambda i: (i // half, 0, 0)),
                  small(dl)],
        out_specs=[taps_spec, pl.BlockSpec((HYENA_ORDER, 1, HYENA_WIDTH), lambda i: (0, 0, 0))],
        out_shape=[jax.ShapeDtypeStruct(taps_shape, F32), jax.ShapeDtypeStruct((HYENA_ORDER, 1, HYENA_WIDTH), F32)],
        compiler_params=_params(("arbitrary",)), name="hyena_filter",
    )(z, w1p, b1r, w2, b2r, frr, w3d, dl)


def _dft_mats(n1, n2):
    n = n1 * n2
    a1 = 2 * np.pi * np.outer(np.arange(n1), np.arange(n1)) / n1
    c1, s1 = np.cos(a1), np.sin(a1)
    a2 = 2 * np.pi * np.outer(np.arange(n2), np.arange(n2)) / n2
    c2, s2 = np.cos(a2), np.sin(a2)
    h = n1 // 2
    mats = dict(
        sig=np.block([[c1[:, :h], s1[:, :h]], [-s1[:, :h], c1[:, :h]]]),
        filt=np.concatenate([c1, -s1], axis=0),
        fwd2=np.block([[c2, s2], [-s2, c2]]),
        inv2=np.block([[c2, -s2], [s2, c2]]),
        last=np.block([[c1[:h, :], -s1[:h, :]], [s1[:h, :], c1[:h, :]]]) / n,
    )
    mats = {k: jnp.asarray(v, F32).astype(BF16) for k, v in mats.items()}
    at = (jnp.arange(n1, dtype=jnp.int32)[:, None] * jnp.arange(n2, dtype=jnp.int32)[None, :]).astype(F32) * (2 * math.pi / n)
    tw = tuple(jnp.broadcast_to(f(at)[:, :, None], (n1, n2, LANES)) for f in (jnp.cos, jnp.sin))
    return mats, tw


def _stage1_kernel(m_ref, x_ref, o_ref, *, rows_in, rows_out, complex_in):
    for r in range(SUB):
        sl = pl.ds(r, rows_in, stride=SUB)
        if complex_in:
            x = jnp.concatenate([x_ref[0, sl, :], x_ref[1, sl, :]], axis=0)
        else:
            x = x_ref[sl, :]
        o_ref[pl.ds(r, rows_out, stride=SUB), :] = _dot(m_ref[...], x.astype(BF16))


def _stage1_call(mat, x, col_block, *, complex_in, name):
    w = HYENA_WIDTH
    nl = w // LANES
    if complex_in:
        _, groups, rows8, _ = x.shape
        spec = pl.BlockSpec((2, None, rows8, LANES), lambda j, c: (0, j, 0, col_block * nl + c))
    else:
        groups, rows8, _ = x.shape
        spec = pl.BlockSpec((None, rows8, LANES), lambda j, c: (j, 0, c))
    rows_out = mat.shape[0]
    kern = functools.partial(_stage1_kernel, rows_in=rows8 // SUB, rows_out=rows_out, complex_in=complex_in)
    return pl.pallas_call(
        kern, grid=(groups, nl),
        in_specs=[pl.BlockSpec(mat.shape, lambda j, c: (0, 0)), spec],
        out_specs=pl.BlockSpec((None, rows_out * SUB, LANES), lambda j, c: (j, 0, c)),
        out_shape=jax.ShapeDtypeStruct((groups, rows_out * SUB, w), F32),
        compiler_params=_params(("parallel", "parallel")), name=name,
    )(mat, x)


def _lane_tile(t):
    return jnp.concatenate([t] * (HYENA_WIDTH // LANES), axis=-1)


def _twiddle_dft2(ar, ai, tc, ts, f_ref, n2):
    br = ar * tc + ai * ts
    bi = ai * tc - ar * ts
    x = _dot(f_ref[...], jnp.concatenate([br, bi], axis=0).astype(BF16))
    return x[:n2], x[n2:]


def _spectrum_kernel(f_ref, ar_ref, ai_ref, tc_ref, ts_ref, ss_ref, hr_ref, hi_ref, *, n2):
    tc, ts = _lane_tile(tc_ref[...]), _lane_tile(ts_ref[...])
    w = ar_ref.shape[-1]
    xr, xi = _twiddle_dft2(ar_ref[...].reshape(n2, w), ai_ref[...].reshape(n2, w), tc, ts, f_ref, n2)
    norm = lax.rsqrt(ss_ref[...] + EPS)
    hr_ref[...] = xr * norm
    hi_ref[...] = xi * norm


def _spectrum_call(fwd2, a, twc, tws, ss, order):
    groups, rows8, w = a.shape
    n2, n1 = groups * SUB, rows8 // (2 * SUB)
    col = lambda off: pl.BlockSpec((groups, SUB, w), lambda k: (0, k + off, 0))
    tw = pl.BlockSpec((None, n2, LANES), lambda k: (k, 0, 0))
    out = pl.BlockSpec((None, n2, w), lambda k: (k, 0, 0))
    return pl.pallas_call(
        functools.partial(_spectrum_kernel, n2=n2), grid=(n1,),
        in_specs=[pl.BlockSpec(fwd2.shape, lambda k: (0, 0)), col(0), col(n1), tw, tw,
                  pl.BlockSpec((None, 1, w), lambda k: (order, 0, 0))],
        out_specs=[out, out],
        out_shape=[jax.ShapeDtypeStruct((n1, n2, w), F32)] * 2,
        compiler_params=_params(("parallel",)), name="hyena_filter_spectrum",
    )(fwd2, a, a, twc, tws, ss)


def _mid_kernel(f_ref, g_ref, ar_ref, ai_ref, tc_ref, ts_ref, hr_ref, hi_ref, or_ref, oi_ref, *, n2):
    tc, ts = _lane_tile(tc_ref[...]), _lane_tile(ts_ref[...])
    w = ar_ref.shape[-1]
    xr, xi = _twiddle_dft2(ar_ref[...].reshape(n2, w), ai_ref[...].reshape(n2, w), tc, ts, f_ref, n2)
    hr, hi = hr_ref[...], hi_ref[...]
    yr = xr * hr - xi * hi
    yi = xr * hi + xi * hr
    y = _dot(g_ref[...], jnp.concatenate([yr, yi], axis=0).astype(BF16))
    cr, ci = y[:n2], y[n2:]
    or_ref[...] = (cr * tc - ci * ts).reshape(or_ref.shape)
    oi_ref[...] = (ci * tc + cr * ts).reshape(oi_ref.shape)


def _mid_call(fwd2, inv2, a, twc, tws, hr, hi):
    groups, rows8, w = a.shape
    n2, n1 = groups * SUB, rows8 // (2 * SUB)
    col = lambda off: pl.BlockSpec((groups, SUB, w), lambda k: (0, k + off, 0))
    tw = pl.BlockSpec((None, n2, LANES), lambda k: (k, 0, 0))
    hs = pl.BlockSpec((None, n2, w), lambda k: (k, 0, 0))
    mat = pl.BlockSpec(fwd2.shape, lambda k: (0, 0))
    return pl.pallas_call(
        functools.partial(_mid_kernel, n2=n2), grid=(n1,),
        in_specs=[mat, mat, col(0), col(n1), tw, tw, hs, hs],
        out_specs=[col(0)] * 2,
        out_shape=[jax.ShapeDtypeStruct((groups, n1 * SUB, w), F32)] * 2,
        compiler_params=_params(("parallel",)), name="hyena_conv_mid",
    )(fwd2, inv2, a, a, twc, tws, hr, hi)


def _last_kernel(m_ref, br_ref, bi_ref, z_ref, x_ref, skip_ref, o_ref, *, n1):
    skip = skip_ref[...]
    half = n1 // 2
    for r in range(SUB):
        full, part = pl.ds(r, n1, stride=SUB), pl.ds(r, half, stride=SUB)
        y = _dot(m_ref[...], jnp.concatenate([br_ref[full, :], bi_ref[full, :]], axis=0).astype(BF16))
        o_ref[0, part, :] = x_ref[0, part, :] * (y[:half] + z_ref[0, part, :] * skip)
        o_ref[1, part, :] = x_ref[1, part, :] * (y[half:] + z_ref[1, part, :] * skip)


def _last_call(mat, br, bi, zsrc, zcol, xsrc, xcol, skip, order):
    groups, rows8, w = br.shape
    n1 = rows8 // SUB
    nl = w // LANES
    b_spec = pl.BlockSpec((None, rows8, LANES), lambda j, c: (j, 0, c))
    seq = lambda col: pl.BlockSpec((2, None, rows8 // 2, LANES), lambda j, c: (0, j, 0, col * nl + c))
    return pl.pallas_call(
        functools.partial(_last_kernel, n1=n1), grid=(groups, nl),
        in_specs=[pl.BlockSpec(mat.shape, lambda j, c: (0, 0)), b_spec, b_spec, seq(zcol), seq(xcol),
                  pl.BlockSpec((None, 1, LANES), lambda j, c: (order, 0, c))],
        out_specs=seq(0),
        out_shape=jax.ShapeDtypeStruct((2, groups, rows8 // 2, w), F32),
        compiler_params=_params(("parallel", "parallel")), name="hyena_conv_last",
    )(mat, br, bi, zsrc, xsrc, skip)


def _hyena_long(uct, taps_t, ss, skip):
    n2 = uct.shape[1] * SUB
    mats, (twc, tws) = _dft_mats(FFT_N1, n2)
    skip3 = skip.reshape(HYENA_ORDER, 1, HYENA_WIDTH)
    z, zcol = uct, 0
    for o in range(HYENA_ORDER):
        af = _stage1_call(mats["filt"], taps_t[o], 0, complex_in=False, name="hyena_filter_dft1")
        hr, hi = _spectrum_call(mats["fwd2"], af, twc, tws, ss, o)
        a = _stage1_call(mats["sig"], z, zcol, complex_in=True, name="hyena_conv_first")
        br, bi = _mid_call(mats["fwd2"], mats["inv2"], a, twc, tws, hr, hi)
        z = _last_call(mats["last"], br, bi, z, zcol, uct, o + 1, skip3, o)
        zcol = 0
    return z


def _ctx_hyena_kernel(v_ref, x1_ref, x2_ref, taps_ref, ss_ref, skip_ref, fh_ref, fz_ref, fi_ref, o_ref, *, n):
    nn = 2 * n
    stack = lambda r: jnp.concatenate([r[0], r[1]], axis=0)
    z = stack(v_ref)
    for o, x_ref in enumerate((x1_ref, x2_ref)):
        h = _dot(fh_ref[...], taps_ref[o], HIGHEST) * lax.rsqrt(ss_ref[o] + EPS)
        hr, hi = h[:nn], h[nn:]
        zf = _dot(fz_ref[...], z, HIGHEST)
        zr, zi = zf[:nn], zf[nn:]
        y = _dot(fi_ref[...], jnp.concatenate([zr * hr - zi * hi, zr * hi + zi * hr], axis=0), HIGHEST)
        z = stack(x_ref) * (y + z * skip_ref[o])
    o_ref[0] = z[:n].astype(o_ref.dtype)
    o_ref[1] = z[n:].astype(o_ref.dtype)


def _ctx_hyena_call(u, taps, ss, skip):
    _, n, _ = u.shape
    nn = 2 * n
    ang = 2 * np.pi * np.outer(np.arange(nn), np.arange(nn)) / nn
    c, s = np.cos(ang), np.sin(ang)
    fh = jnp.asarray(np.concatenate([c, -s], axis=0), F32)
    fz = jnp.asarray(np.block([[c[:, :n], s[:, :n]], [-s[:, :n], c[:, :n]]]), F32)
    fi = jnp.asarray(np.block([[c[:n, :], -s[:n, :]], [s[:n, :], c[:n, :]]]) / nn, F32)
    nchunk = HYENA_WIDTH // LANES
    skip3 = skip.reshape(HYENA_ORDER, 1, HYENA_WIDTH)
    small = lambda a: pl.BlockSpec(a.shape, lambda j: (0,) * a.ndim)
    part = lambda g: pl.BlockSpec((2, n, LANES), lambda j, g=g: (0, 0, g * nchunk + j))

    return pl.pallas_call(
        functools.partial(_ctx_hyena_kernel, n=n), grid=(nchunk,),
        in_specs=[part(0), part(1), part(2),
                  pl.BlockSpec((HYENA_ORDER, nn, LANES), lambda j: (0, 0, j)),
                  pl.BlockSpec((HYENA_ORDER, 1, LANES), lambda j: (0, 0, j)),
                  pl.BlockSpec((HYENA_ORDER, 1, LANES), lambda j: (0, 0, j)),
                  small(fh), small(fz), small(fi)],
        out_specs=pl.BlockSpec((2, n, LANES), lambda j: (0, 0, j)),
        out_shape=jax.ShapeDtypeStruct((2, n, HYENA_WIDTH), BF16),
        compiler_params=_params(("parallel",)), name="hyena_ctx",
    )(u, u, u, taps, ss, skip3, fh, fz, fi)


def _outproj_kernel(x_ref, ya_ref, yb_ref, g_ref, w_ref, gate_ref, o_ref):
    tm, half = ya_ref.shape
    ya = (ya_ref[...].astype(F32) * g_ref[:, :half].astype(F32)).astype(BF16)
    yb = (yb_ref[...].reshape(tm, half).astype(F32) * g_ref[:, half:].astype(F32)).astype(BF16)
    upd = _dot(ya, w_ref[:half, :]) + _dot(yb, w_ref[half:, :])
    o_ref[...] = x_ref[...] + gate_ref[...] * upd


def _outproj_call(x2d, ya, yb, gates, w_bf, gate_vec, *, seq_len, tm, per_batch_mod, yb_t_layout=False):
    r, d = x2d.shape
    tps = seq_len // tm
    half = ya.shape[-1]
    if yb_t_layout:
        yb_spec = pl.BlockSpec((None, tm // SUB, SUB, half), lambda i: (i // tps, 0, i % tps, 0))
    else:
        yb_spec = pl.BlockSpec((tm, half), lambda i: (i, 0))
    mod_map = (lambda i: (i // tps, 0, 0)) if per_batch_mod else (lambda i: (0, 0, 0))
    return pl.pallas_call(
        _outproj_kernel, grid=(r // tm,),
        in_specs=[pl.BlockSpec((tm, d), lambda i: (i, 0)), pl.BlockSpec((tm, half), lambda i: (i, 0)), yb_spec,
                  pl.BlockSpec((tm, 2 * half), lambda i: (i, 0)), pl.BlockSpec(w_bf.shape, lambda i: (0, 0)),
                  pl.BlockSpec((None, 1, d), mod_map)],
        out_specs=pl.BlockSpec((tm, d), lambda i: (i, 0)),
        out_shape=jax.ShapeDtypeStruct((r, d), F32),
        compiler_params=_params(("parallel",)), name="outproj",
    )(x2d, ya, yb, gates, w_bf, gate_vec)


def _rope_tables(n_tokens):
    t = jnp.arange(n_tokens, dtype=jnp.int32)
    row = (t // GRID_W).astype(F32)
    col = (t % GRID_W).astype(F32)
    n_pairs = HEAD_DIM // 4
    inv = ROPE_THETA ** (-jnp.arange(n_pairs, dtype=F32) / n_pairs)
    ang = jnp.concatenate([row[:, None] * inv, col[:, None] * inv], axis=-1)
    cos = jnp.repeat(jnp.cos(ang), 2, axis=-1)
    sin = jnp.repeat(jnp.sin(ang), 2, axis=-1)
    even = (jnp.arange(HEAD_DIM) % 2 == 0)[None, :]
    tile = lambda a: jnp.tile(a, (1, LANES // HEAD_DIM))
    return tile(cos), tile(jnp.where(even, -sin, 0.0)), tile(jnp.where(even, 0.0, sin))


def _gain_rows(*gains):
    rows = [jnp.tile(g, LANES // HEAD_DIM) for g in gains]
    rows += [jnp.zeros((LANES,), F32)] * (8 - len(rows))
    return jnp.stack(rows)


def kernel(x, c, ctx, c_ctx, norm_g, w_ada, b_ada, w_in, w_out, glob_q_norm, glob_k_norm, hy_conv_w, hy_conv_b, hy_w1, hy_b1, hy_w2, hy_b2, hy_freq, hy_w3, hy_skip, win_q_norm, win_k_norm, win_sink, nat_q_norm, nat_k_norm, nat_rpb):
    b, s, d = x.shape
    n_ctx = ctx.shape[1]
    n2 = 2 * s // FFT_N1
    tm = n2
    w_in_bf, w_out_bf = w_in.astype(BF16), w_out.astype(BF16)

    cond = jnp.zeros((8, d), F32).at[:b].set(c).at[b].set(c_ctx)
    mod = _ada_call(cond, w_ada, b_ada)
    shift, scale, gate = (mod[:, :, i * d:(i + 1) * d] for i in range(3))
    lat = lambda m, l: m[l, :b].reshape(b, 1, d)
    cx = lambda m, l: m[l, b:b + 1].reshape(1, 1, d)

    rope = _rope_tables(s)
    x2 = x.reshape(b * s, d)
    xc2 = ctx.reshape(b * n_ctx, d)
    heads4 = lambda a, n: a.reshape(a.shape[0], b, n, HEAD_DIM)

    gains0 = _gain_rows(glob_q_norm[0], glob_k_norm[0])
    gw, kw = GROUP_WIDTH, N_KV_HEADS * HEAD_DIM
    hyw = (HYENA_ORDER + 1) * HYENA_WIDTH
    plan_lat = (("qk", 0, gw, 0, True, QK_SCALE), ("qk", gw, kw, 1, True, 1.0), ("v", gw + kw, kw),
                ("hy", gw + 2 * kw, hyw), ("gate", gw + 2 * kw + hyw, 2 * gw))
    plan_ctx = (("qk", 0, gw, 0, False, QK_SCALE), ("qk", gw, kw, 1, False, 1.0), ("v", gw + kw, kw),
                ("hy", gw + 2 * kw, hyw), ("gate", gw + 2 * kw + hyw, 2 * gw))
    conv = (hy_conv_w[0], hy_conv_b[0])
    q0, k0, v0, uct, g0 = _inproj_call(
        x2, lat(shift, 0), lat(scale, 0), norm_g[0], w_in_bf[0], plan_lat, seq_len=s, tm=tm, per_batch_mod=True,
        rope_tabs=rope, gains=gains0, conv=conv, hy_t_layout=True, name="inproj_even")
    qc, kc, vc, ucc, gc = _inproj_call(
        xc2, cx(shift, 0), cx(scale, 0), norm_g[0], w_in_bf[0], plan_ctx, seq_len=n_ctx, tm=n_ctx,
        per_batch_mod=False, gains=gains0, conv=conv, name="inproj_even_ctx")
    kc4, vc4 = heads4(kc, n_ctx), heads4(vc, n_ctx)
    ya = _flash_call(heads4(q0, s), kc4, vc4, heads4(k0, s), heads4(v0, s), tq=256, name="global_gqa")
    yca = _flash_call(heads4(qc, n_ctx), kc4, vc4, tq=n_ctx, name="ctx_gqa")

    filt = (hy_w1[0], hy_b1[0], hy_w2[0], hy_b2[0], hy_freq[0], hy_w3[0])
    taps_t, ss = _filter_call(s, *filt, tm=n2, t_layout=True)
    ybt = _hyena_long(uct, taps_t, ss, hy_skip[0])
    taps_c, ss_c = _filter_call(n_ctx, *filt, tm=n_ctx, t_layout=False)
    ycb = _ctx_hyena_call(ucc.reshape(b, n_ctx, hyw), taps_c, ss_c, hy_skip[0])

    x2 = _outproj_call(x2, ya.reshape(b * s, gw), ybt, g0, w_out_bf[0], lat(gate, 0), seq_len=s, tm=tm,
                       per_batch_mod=True, yb_t_layout=True)
    xc2 = _outproj_call(xc2, yca.reshape(b * n_ctx, gw), ycb.reshape(b * n_ctx, gw), gc, w_out_bf[0], cx(gate, 0),
                        seq_len=n_ctx, tm=n_ctx, per_batch_mod=False)

    gains1 = _gain_rows(win_q_norm[0], win_k_norm[0], nat_q_norm[0], nat_k_norm[0])
    o_nq = gw + 2 * kw
    plan_lat = (("qk", 0, gw, 0, True, QK_SCALE), ("qk", gw, kw, 1, True, 1.0), ("v", gw + kw, kw),
                ("qk", o_nq, gw, 2, False, QK_SCALE), ("qk", o_nq + gw, gw, 3, False, 1.0), ("v", o_nq + 2 * gw, gw),
                ("gate", o_nq + 3 * gw, 2 * gw))
    plan_ctx = (("qk", gw, kw, 1, False, 1.0), ("v", gw + kw, kw),
                ("qk", o_nq + gw, gw, 3, False, 1.0), ("v", o_nq + 2 * gw, gw))
    qw, kwl, vwl, qn, kn, vn, g1 = _inproj_call(
        x2, lat(shift, 1), lat(scale, 1), norm_g[1], w_in_bf[1], plan_lat, seq_len=s, tm=tm, per_batch_mod=True,
        rope_tabs=rope, gains=gains1, name="inproj_odd")
    kwc, vwc, knc, vnc = _inproj_call(
        xc2, cx(shift, 1), cx(scale, 1), norm_g[1], w_in_bf[1], plan_ctx, seq_len=n_ctx, tm=n_ctx,
        per_batch_mod=False, gains=gains1, name="inproj_odd_ctx")
    yw = _window_call(win_sink[0], heads4(qw, s), heads4(kwc, n_ctx), heads4(vwc, n_ctx), heads4(kwl, s), heads4(vwl, s))
    table = _rpb_table_call(nat_rpb[0])
    yn = _na_call(heads4(qn, s), heads4(kn, s), heads4(vn, s), heads4(knc, n_ctx), heads4(vnc, n_ctx), table)
    x2 = _outproj_call(x2, yw.reshape(b * s, gw), yn.reshape(b * s, gw), g1, w_out_bf[1], lat(gate, 1), seq_len=s,
                       tm=tm, per_batch_mod=True)
    return x2.reshape(b, s, d)
```

```python
import functools
import math

import numpy as np
import jax
import jax.numpy as jnp
from jax import lax
from jax.experimental import pallas as pl
from jax.experimental.pallas import tpu as pltpu

F32 = jnp.float32
BF16 = jnp.bfloat16
HIGHEST = lax.Precision.HIGHEST

GRID_W = 64
HEAD_DIM = 64
GROUP_WIDTH = 512
N_Q_HEADS = 8
N_KV_HEADS = 2
GQA_GROUP = N_Q_HEADS // N_KV_HEADS
HYENA_WIDTH = 512
HYENA_ORDER = 2
HYENA_EMB = 33
HYENA_HIDDEN = 64
HYENA_DECAY_TARGET = 1e-2
HYENA_SHORT_DECAY_PCT = 0.3
HYENA_LONG_DECAY_PCT = 1.5
WINDOW = 128
NA_ROWS = 8
NA_COLS = 16
ROPE_THETA = 10000.0
EPS = 1e-6
NEG_INF = -1e30
QK_SCALE = HEAD_DIM ** -0.5

LANES = 128
SUB = 8
HALO = 16
FFT_N1 = 128
FLASH_UNROLL = 8
VMEM_LIMIT = 56 * 1024 * 1024


def _dot(a, b, precision=None):
    return jnp.dot(a, b, preferred_element_type=F32, precision=precision)


def _dot_nt(a, b):
    return lax.dot_general(a, b, (((1,), (1,)), ((), ())), preferred_element_type=F32)


def _params(sem):
    return pltpu.CompilerParams(dimension_semantics=sem, vmem_limit_bytes=VMEM_LIMIT)


def _silu(v):
    return v / (1.0 + jnp.exp(-v))


def _ada_kernel(cond_ref, w_ref, b_ref, o_ref):
    o_ref[0] = _dot(_silu(cond_ref[...]), w_ref[0], HIGHEST) + b_ref[0]


def _ada_call(cond, w_ada, b_ada):
    depth, d, d3 = w_ada.shape
    tn = 1024
    return pl.pallas_call(
        _ada_kernel,
        grid=(depth, d3 // tn),
        in_specs=[pl.BlockSpec((8, d), lambda l, j: (0, 0)),
                  pl.BlockSpec((1, d, tn), lambda l, j: (l, 0, j)),
                  pl.BlockSpec((1, 1, tn), lambda l, j: (l, 0, j))],
        out_specs=pl.BlockSpec((1, 8, tn), lambda l, j: (l, 0, j)),
        out_shape=jax.ShapeDtypeStruct((depth, 8, d3), F32),
        compiler_params=_params(("parallel", "parallel")),
        name="adaln",
    )(cond, w_ada, b_ada.reshape(depth, 1, d3))


def _inproj_kernel(*refs, plan, tm, tps, has_halo, has_rope):
    it = iter(refs)
    x_ref = next(it)
    if has_halo:
        xp_ref, xn_ref = next(it), next(it)
    sh_ref, sc_ref, g_ref, w_ref = next(it), next(it), next(it), next(it)
    if has_rope:
        cos_ref, s1_ref, s2_ref = next(it), next(it), next(it)
    gains_ref, mavg_ref = next(it), next(it)
    if has_halo:
        cw_ref, cb_ref = next(it), next(it)
    outs = [next(it) for _ in plan]
    if has_halo:
        xe_ref, hu_ref = next(it), next(it)

    g, sh, sc = g_ref[...], sh_ref[...], sc_ref[...]

    def modnorm(xv):
        ms = jnp.mean(xv * xv, axis=-1, keepdims=True)
        return (xv * lax.rsqrt(ms + EPS) * g) * (1.0 + sc) + sh

    h = modnorm(x_ref[...]).astype(BF16)
    mavg = mavg_ref[...]

    for seg, o_ref in zip(plan, outs):
        kind, start, width = seg[0], seg[1], seg[2]
        if kind == "hy":
            xe_ref[0:HALO] = modnorm(xp_ref[...]).astype(BF16)
            xe_ref[HALO:HALO + tm] = h
            xe_ref[HALO + tm:HALO + tm + HALO] = modnorm(xn_ref[...]).astype(BF16)
            hu_ref[...] = _dot(xe_ref[...], w_ref[:, start:start + width])
            pos = pl.program_id(0) % tps
            row = lax.broadcasted_iota(jnp.int32, (tm, 1), 0)
            kill_prev = jnp.where(pos == 0, 0, -1)
            kill_next = jnp.where(pos == tps - 1, tm - 1, -1)
            a = jnp.where(row == kill_prev, 0.0, hu_ref[pl.ds(HALO - 1, tm), :])
            b = hu_ref[pl.ds(HALO, tm), :]
            c = jnp.where(row == kill_next, 0.0, hu_ref[pl.ds(HALO + 1, tm), :])
            uc = a * cw_ref[0:1, :] + b * cw_ref[1:2, :] + c * cw_ref[2:3, :] + cb_ref[...]
            o_ref[...] = uc.reshape(o_ref.shape)
            continue
        pv = _dot(h, w_ref[:, start:start + width])
        if kind == "gate":
            o_ref[...] = _silu(pv).astype(o_ref.dtype)
        elif kind == "v":
            for hd in range(width // HEAD_DIM):
                o_ref[hd] = pv[:, hd * HEAD_DIM:(hd + 1) * HEAD_DIM].astype(o_ref.dtype)
        elif kind == "vones":
            lane = lax.broadcasted_iota(jnp.int32, (tm, LANES - HEAD_DIM), 1)
            tail = jnp.where(lane == 0, 1.0, 0.0)
            for hd in range(width // HEAD_DIM):
                o_ref[hd] = jnp.concatenate([pv[:, hd * HEAD_DIM:(hd + 1) * HEAD_DIM], tail], axis=1).astype(o_ref.dtype)
        else:
            gain_row, rope, scale = seg[3], seg[4], seg[5]
            gain = gains_ref[gain_row:gain_row + 1, :]
            for c in range(width // LANES):
                v = pv[:, c * LANES:(c + 1) * LANES]
                sq = v * v
                hi = sq.astype(BF16)
                lo = (sq - hi.astype(F32)).astype(BF16)
                ms = _dot(hi, mavg) + _dot(lo, mavg)
                vn = v * lax.rsqrt(ms + EPS) * gain
                if rope:
                    vn = (vn * cos_ref[...] + pltpu.roll(vn, LANES - 1, 1) * s1_ref[...]
                          + pltpu.roll(vn, 1, 1) * s2_ref[...])
                vn = (vn * scale).astype(o_ref.dtype)
                o_ref[2 * c] = vn[:, :HEAD_DIM]
                o_ref[2 * c + 1] = vn[:, HEAD_DIM:]


def _inproj_call(x2d, shift, scale, norm_g, w_bf, plan, *, seq_len, tm, per_batch_mod,
                 rope_tabs=None, gains=None, conv=None, hy_t_layout=False, name="inproj"):
    r, d = x2d.shape
    tps = seq_len // tm
    n_tiles = r // tm
    n_seq = r // seq_len
    has_halo = any(s[0] == "hy" for s in plan)
    has_rope = rope_tabs is not None
    hb = tm // HALO
    mod_map = (lambda i: (i // tps, 0, 0)) if per_batch_mod else (lambda i: (0, 0, 0))

    args, specs = [x2d], [pl.BlockSpec((tm, d), lambda i: (i, 0))]
    if has_halo:
        args += [x2d, x2d]
        specs += [pl.BlockSpec((HALO, d), lambda i: (jnp.maximum(i * hb - 1, 0), 0)),
                  pl.BlockSpec((HALO, d), lambda i: (jnp.minimum((i + 1) * hb, r // HALO - 1), 0))]
    args += [shift, scale, norm_g.reshape(1, d), w_bf]
    specs += [pl.BlockSpec((None, 1, d), mod_map), pl.BlockSpec((None, 1, d), mod_map),
              pl.BlockSpec((1, d), lambda i: (0, 0)),
              pl.BlockSpec(w_bf.shape, lambda i: (0, 0))]
    if has_rope:
        args += list(rope_tabs)
        specs += [pl.BlockSpec((tm, LANES), lambda i: (i % tps, 0))] * 3
    mavg = np.kron(np.eye(LANES // HEAD_DIM), np.full((HEAD_DIM, HEAD_DIM), 1.0 / HEAD_DIM))
    args += [gains, jnp.asarray(mavg, BF16)]
    specs += [pl.BlockSpec(gains.shape, lambda i: (0, 0)), pl.BlockSpec((LANES, LANES), lambda i: (0, 0))]
    if has_halo:
        cw, cb = conv
        args += [cw, cb.reshape(1, -1)]
        specs += [pl.BlockSpec(cw.shape, lambda i: (0, 0)), pl.BlockSpec((1, cb.shape[-1]), lambda i: (0, 0))]

    out_shapes, out_specs = [], []
    for seg in plan:
        kind, width = seg[0], seg[2]
        if kind == "hy":
            if hy_t_layout:
                out_shapes.append(jax.ShapeDtypeStruct((n_seq, tm // SUB, tps * SUB, width), F32))
                out_specs.append(pl.BlockSpec((None, tm // SUB, SUB, width), lambda i: (i // tps, 0, i % tps, 0)))
            else:
                out_shapes.append(jax.ShapeDtypeStruct((r, width), F32))
                out_specs.append(pl.BlockSpec((tm, width), lambda i: (i, 0)))
        elif kind == "gate":
            out_shapes.append(jax.ShapeDtypeStruct((r, width), BF16))
            out_specs.append(pl.BlockSpec((tm, width), lambda i: (i, 0)))
        else:
            nh = width // HEAD_DIM
            hw = LANES if kind == "vones" else HEAD_DIM
            out_shapes.append(jax.ShapeDtypeStruct((nh, r, hw), BF16))
            out_specs.append(pl.BlockSpec((nh, tm, hw), lambda i: (0, i, 0)))
    scratch = []
    if has_halo:
        hw = [s[2] for s in plan if s[0] == "hy"][0]
        scratch = [pltpu.VMEM((tm + 2 * HALO, d), BF16), pltpu.VMEM((tm + 2 * HALO, hw), F32)]
    kern = functools.partial(_inproj_kernel, plan=plan, tm=tm, tps=tps, has_halo=has_halo, has_rope=has_rope)
    return pl.pallas_call(
        kern, grid=(n_tiles,), in_specs=specs, out_specs=out_specs, out_shape=out_shapes,
        scratch_shapes=scratch, compiler_params=_params(("parallel",)), name=name,
    )(*args)


def _flash_kernel(q_ref, kc_ref, vc_ref, *rest, tq, tk, n_lat_chunks):
    if n_lat_chunks:
        kl_ref, vl_ref, o_ref, m_ref, acc_ref, s_ref = rest
    else:
        o_ref, m_ref, acc_ref = rest
    rows = GQA_GROUP * tq
    q = q_ref[...].reshape(rows, HEAD_DIM)

    def softmax_pv(s, vb, first):
        smax = jnp.max(s, axis=1, keepdims=True)
        if first:
            m_new = jnp.broadcast_to(smax, (rows, LANES))
        else:
            m_prev = m_ref[...]
            m_new = jnp.maximum(m_prev, smax)
        m_ref[...] = m_new
        p = jnp.exp2(s - jnp.concatenate([m_new] * (s.shape[1] // LANES), axis=1)).astype(BF16)
        pv = _dot(p, vb)
        if first:
            acc_ref[...] = pv
        else:
            acc_ref[...] = jnp.exp2(m_prev - m_new) * acc_ref[...] + pv

    softmax_pv(_dot_nt(q, kc_ref[...]), vc_ref[...], True)
    if n_lat_chunks:
        s_ref[0] = _dot_nt(q, kl_ref[pl.ds(0, tk), :])

        def half(j, cur):
            nxt = pl.multiple_of(jnp.minimum(j + 1, n_lat_chunks - 1) * tk, tk)
            s_ref[1 - cur] = _dot_nt(q, kl_ref[pl.ds(nxt, tk), :])
            softmax_pv(s_ref[cur], vl_ref[pl.ds(pl.multiple_of(j * tk, tk), tk), :], False)

        def body(i, carry):
            for u in range(FLASH_UNROLL):
                half(FLASH_UNROLL * i + u, u % 2)
            return carry
        lax.fori_loop(0, n_lat_chunks // FLASH_UNROLL, body, 0)
    acc = acc_ref[...]
    o = acc[:, :HEAD_DIM] / acc[:, HEAD_DIM:HEAD_DIM + 1]
    o_ref[...] = jnp.concatenate([o[gi * tq:(gi + 1) * tq] for gi in range(GQA_GROUP)], axis=-1).astype(o_ref.dtype)


def _flash_call(q, kc, vc, kl=None, vl=None, *, tq, tk=512, name="gqa"):
    _, b, sq, _ = q.shape
    c = kc.shape[2]
    kv_spec = lambda n, w: pl.BlockSpec((None, None, n, w), lambda bi, j, i: (j, bi, 0, 0))
    args = [q, kc, vc]
    specs = [pl.BlockSpec((GQA_GROUP, None, tq, HEAD_DIM), lambda bi, j, i: (j, bi, i, 0)),
             kv_spec(c, HEAD_DIM), kv_spec(c, LANES)]
    rows = GQA_GROUP * tq
    scratch = [pltpu.VMEM((rows, LANES), F32), pltpu.VMEM((rows, LANES), F32)]
    n_lat = 0
    if kl is not None:
        s = kl.shape[2]
        n_lat = s // tk
        assert n_lat % FLASH_UNROLL == 0 and FLASH_UNROLL % 2 == 0
        args += [kl, vl]
        specs += [kv_spec(s, HEAD_DIM), kv_spec(s, LANES)]
        scratch.append(pltpu.VMEM((2, rows, tk), F32))
    kern = functools.partial(_flash_kernel, tq=tq, tk=tk, n_lat_chunks=n_lat)
    return pl.pallas_call(
        kern, grid=(b, N_KV_HEADS, sq // tq), in_specs=specs,
        out_specs=pl.BlockSpec((None, tq, GQA_GROUP * HEAD_DIM), lambda bi, j, i: (bi, i, j)),
        out_shape=jax.ShapeDtypeStruct((b, sq, GROUP_WIDTH), BF16),
        scratch_shapes=scratch,
        compiler_params=_params(("parallel", "parallel", "arbitrary")), name=name,
    )(*args)


def _window_kernel(sink_ref, q_ref, kc_ref, vc_ref, kl_ref, vl_ref, o_ref, *, tq, span, seq):
    j, i = pl.program_id(1), pl.program_id(2)
    start = pl.multiple_of(jnp.clip(i * tq - WINDOW, 0, seq - span), LANES)
    kb, vb = kl_ref[pl.ds(start, span), :], vl_ref[pl.ds(start, span), :]
    kc, vc = kc_ref[...], vc_ref[...]
    qpos = i * tq + lax.broadcasted_iota(jnp.int32, (tq, span), 0)
    kpos = start + lax.broadcasted_iota(jnp.int32, (tq, span), 1)
    valid = jnp.abs(qpos - kpos) <= WINDOW
    outs = []
    for gi in range(GQA_GROUP):
        q = q_ref[gi]
        s_loc = jnp.where(valid, _dot_nt(q, kb), NEG_INF)
        s_ctx = _dot_nt(q, kc)
        sk = sink_ref[j * GQA_GROUP + gi]
        m = jnp.maximum(jnp.maximum(jnp.max(s_loc, axis=-1, keepdims=True),
                                    jnp.max(s_ctx, axis=-1, keepdims=True)), sk)
        p_loc, p_ctx = jnp.exp(s_loc - m), jnp.exp(s_ctx - m)
        l = (jnp.sum(p_loc, axis=-1, keepdims=True) + jnp.sum(p_ctx, axis=-1, keepdims=True)
             + jnp.exp(sk - m))
        outs.append((_dot(p_loc.astype(BF16), vb) + _dot(p_ctx.astype(BF16), vc)) / l)
    o_ref[...] = jnp.concatenate(outs, axis=-1).astype(o_ref.dtype)


def _window_call(sink, q, kc, vc, kl, vl, *, tq=256):
    _, b, s, _ = q.shape
    c = kc.shape[2]
    span = tq + 2 * WINDOW
    kern = functools.partial(_window_kernel, tq=tq, span=span, seq=s)
    full = lambda n: pl.BlockSpec((None, None, n, HEAD_DIM), lambda bi, j, i: (j, bi, 0, 0))
    return pl.pallas_call(
        kern, grid=(b, N_KV_HEADS, s // tq),
        in_specs=[pl.BlockSpec(memory_space=pltpu.SMEM),
                  pl.BlockSpec((GQA_GROUP, None, tq, HEAD_DIM), lambda bi, j, i: (j, bi, i, 0)),
                  full(c), full(c), full(s), full(s)],
        out_specs=pl.BlockSpec((None, tq, GQA_GROUP * HEAD_DIM), lambda bi, j, i: (bi, i, j)),
        out_shape=jax.ShapeDtypeStruct((b, s, GROUP_WIDTH), BF16),
        compiler_params=_params(("parallel", "parallel", "arbitrary")), name="window_gqa",
    )(sink, q, kc, vc, kl, vl)


NA_BLOCK = 2 * GRID_W
NA_KEY_BLOCKS = 5
NA_SHIFTS = 9


def _rpb_table_kernel(rpb_ref, o_ref):
    h, ei = pl.program_id(0), pl.program_id(1)
    e = 2 * ei - (NA_SHIFTS - 1)
    sub = lax.broadcasted_iota(jnp.int32, (NA_BLOCK, NA_BLOCK), 0)
    lane = lax.broadcasted_iota(jnp.int32, (NA_BLOCK, NA_BLOCK), 1)
    qc, kc = sub % GRID_W, lane % GRID_W
    dr = e + lane // GRID_W - sub // GRID_W
    dc = kc - qc
    c0 = jnp.clip(qc - NA_COLS // 2, 0, GRID_W - NA_COLS)
    n_dc = 2 * NA_COLS - 1
    per_head = (2 * NA_ROWS - 1) * n_dc
    val = jnp.zeros((NA_BLOCK, NA_BLOCK), F32)
    for ddr in (-1, 0, 1):
        d = e + ddr
        base = h * per_head + (jnp.clip(d, 1 - NA_ROWS, NA_ROWS - 1) + NA_ROWS - 1) * n_dc

        def body(c, v, d=d, base=base):
            return jnp.where(jnp.where(dr == d, dc, n_dc) == c - (NA_COLS - 1), rpb_ref[base + c], v)
        val = lax.fori_loop(0, n_dc, body, val)
    ok = jnp.where(kc >= c0, jnp.where(kc < c0 + NA_COLS, jnp.abs(dr), NA_ROWS), NA_ROWS) < NA_ROWS
    o_ref[0, 0] = jnp.where(ok, val, NEG_INF)


def _rpb_table_call(rpb):
    h = rpb.shape[0]
    return pl.pallas_call(
        _rpb_table_kernel, grid=(h, NA_SHIFTS),
        in_specs=[pl.BlockSpec(memory_space=pltpu.SMEM)],
        out_specs=pl.BlockSpec((1, 1, NA_BLOCK, NA_BLOCK), lambda hh, e: (hh, e, 0, 0)),
        out_shape=jax.ShapeDtypeStruct((h, NA_SHIFTS, NA_BLOCK, NA_BLOCK), F32),
        compiler_params=_params(("parallel", "parallel")), name="rpb_table",
    )(rpb.reshape(-1))


def _na_first_key_row(i, rows):
    return jnp.clip(2 * i - NA_ROWS // 2, 0, rows - 2 * NA_KEY_BLOCKS)


def _na_kernel(*refs, rows):
    q_ref = refs[0]
    k_refs = refs[1:1 + NA_KEY_BLOCKS]
    v_refs = refs[1 + NA_KEY_BLOCKS:1 + 2 * NA_KEY_BLOCKS]
    kc_ref, vc_ref, tab_ref, o_ref = refs[1 + 2 * NA_KEY_BLOCKS:]
    i = pl.program_id(1)
    rs = _na_first_key_row(i, rows)
    sub = lax.broadcasted_iota(jnp.int32, (NA_BLOCK, NA_BLOCK), 0)
    lane = lax.broadcasted_iota(jnp.int32, (NA_BLOCK, NA_BLOCK), 1)
    rq = 2 * i + sub // GRID_W
    r0 = jnp.clip(rq - NA_ROWS // 2, 0, rows - NA_ROWS)
    row_masks, shift_idx = [], []
    for t in range(NA_KEY_BLOCKS):
        off = rs + 2 * t + lane // GRID_W - r0
        row_masks.append(jnp.where(jnp.where(off >= 0, off, NA_ROWS) < NA_ROWS, 0.0, NEG_INF))
        shift_idx.append((rs - 2 * i + 2 * t + NA_SHIFTS - 1) // 2)
    outs = []
    for h in range(N_Q_HEADS):
        q = q_ref[h]
        s_loc = [_dot_nt(q, k_refs[t][h]) + tab_ref[h, shift_idx[t]] + row_masks[t]
                 for t in range(NA_KEY_BLOCKS)]
        s_ctx = _dot_nt(q, kc_ref[h])
        m = jnp.max(s_ctx, axis=-1, keepdims=True)
        for s in s_loc:
            m = jnp.maximum(m, jnp.max(s, axis=-1, keepdims=True))
        p_ctx = jnp.exp(s_ctx - m)
        l = jnp.sum(p_ctx, axis=-1, keepdims=True)
        o = _dot(p_ctx.astype(BF16), vc_ref[h])
        for t in range(NA_KEY_BLOCKS):
            p = jnp.exp(s_loc[t] - m)
            l = l + jnp.sum(p, axis=-1, keepdims=True)
            o = o + _dot(p.astype(BF16), v_refs[t][h])
        outs.append(o / l)
    o_ref[...] = jnp.concatenate(outs, axis=-1).astype(o_ref.dtype)


def _na_call(q, k, v, kc, vc, table):
    nh, b, s, _ = q.shape
    c = kc.shape[2]
    rows = s // GRID_W
    blk = lambda t: pl.BlockSpec(
        (nh, None, NA_BLOCK, HEAD_DIM), lambda bi, i, t=t: (0, bi, _na_first_key_row(i, rows) // 2 + t, 0))
    ctx = pl.BlockSpec((nh, None, c, HEAD_DIM), lambda bi, i: (0, bi, 0, 0))
    return pl.pallas_call(
        functools.partial(_na_kernel, rows=rows), grid=(b, s // NA_BLOCK),
        in_specs=([pl.BlockSpec((nh, None, NA_BLOCK, HEAD_DIM), lambda bi, i: (0, bi, i, 0))]
                  + [blk(t) for t in range(NA_KEY_BLOCKS)] * 2
                  + [ctx, ctx, pl.BlockSpec(table.shape, lambda bi, i: (0, 0, 0, 0))]),
        out_specs=pl.BlockSpec((None, NA_BLOCK, nh * HEAD_DIM), lambda bi, i: (bi, i, 0)),
        out_shape=jax.ShapeDtypeStruct((b, s, nh * HEAD_DIM), BF16),
        compiler_params=_params(("parallel", "arbitrary")), name="nbr_attn",
    )(q, *([k] * NA_KEY_BLOCKS), *([v] * NA_KEY_BLOCKS), kc, vc, table)


def _hyena_features(idx, n):
    bands = (HYENA_EMB - 1) // 2
    idx_f = idx.astype(F32)
    t = idx_f / (n - 1)
    w = 2.0 * math.pi * idx_f / n
    f = jnp.linspace(1e-4, bands - 1, bands, dtype=F32)[None, :]
    fw = f * w[:, None]
    z = jnp.concatenate([t[:, None], jnp.cos(fw), -jnp.sin(fw)], axis=-1)
    return jnp.pad(z, ((0, 0), (0, HYENA_HIDDEN - HYENA_EMB)))


def _filter_kernel(z_ref, w1_ref, b1_ref, w2_ref, b2_ref, fr_ref, w3_ref, dl_ref, taps_ref, ss_ref, *, tm, zero_tile):
    i = pl.program_id(0)
    zf = z_ref[...]
    fr = fr_ref[...]
    h1 = jnp.sin(fr * (_dot(zf, w1_ref[...], HIGHEST) + b1_ref[...]))
    h2 = jnp.sin(fr * (_dot(h1, w2_ref[...], HIGHEST) + b2_ref[...]))
    filt = _dot(h2, w3_ref[...], HIGHEST)
    decay = jnp.exp(-zf[:, 0:1] * dl_ref[...])
    row = lax.broadcasted_iota(jnp.int32, (tm, 1), 0)
    kill = jnp.where(i == zero_tile, 0, -1)

    @pl.when(i == 0)
    def _():
        ss_ref[...] = jnp.zeros_like(ss_ref)

    for o in range(HYENA_ORDER):
        tp = jnp.where(row == kill, 0.0, filt[:, o * HYENA_WIDTH:(o + 1) * HYENA_WIDTH] * decay)
        taps_ref[o] = tp.reshape(taps_ref.shape[1:])
        ss_ref[o] += jnp.sum(tp * tp, axis=0, keepdims=True)


def _filter_call(n, w1, b1, w2, b2, freq, w3, *, tm, t_layout):
    n_tiles = 2 * n // tm
    pos = jnp.arange(2 * n, dtype=jnp.int32)
    idx = jnp.where(pos < n, pos, 2 * n - pos)
    z = _hyena_features(idx, n)
    w1p = jnp.pad(w1, ((0, HYENA_HIDDEN - HYENA_EMB), (0, 0)))
    w3d = w3.reshape(HYENA_HIDDEN, HYENA_ORDER, 2, HYENA_WIDTH).transpose(2, 0, 1, 3).reshape(
        2, HYENA_HIDDEN, HYENA_ORDER * HYENA_WIDTH)
    deltas = np.abs(np.linspace(math.log(HYENA_DECAY_TARGET) / HYENA_LONG_DECAY_PCT,
                                math.log(HYENA_DECAY_TARGET) / HYENA_SHORT_DECAY_PCT, HYENA_WIDTH))
    half = n_tiles // 2
    small = lambda a: pl.BlockSpec(a.shape, lambda i: (0,) * a.ndim)
    b1r, b2r, frr = b1.reshape(1, -1), b2.reshape(1, -1), freq.reshape(1, -1)
    dl = jnp.asarray(deltas.reshape(1, -1), F32)
    if t_layout:
        taps_shape = (HYENA_ORDER, tm // SUB, n_tiles * SUB, HYENA_WIDTH)
        taps_spec = pl.BlockSpec((HYENA_ORDER, tm // SUB, SUB, HYENA_WIDTH), lambda i: (0, 0, i, 0))
    else:
        taps_shape = (HYENA_ORDER, 2 * n, HYENA_WIDTH)
        taps_spec = pl.BlockSpec((HYENA_ORDER, tm, HYENA_WIDTH), lambda i: (0, i, 0))
    return pl.pallas_call(
        functools.partial(_filter_kernel, tm=tm, zero_tile=half), grid=(n_tiles,),
        in_specs=[pl.BlockSpec((tm, HYENA_HIDDEN), lambda i: (i, 0)), small(w1p), small(b1r), small(w2), small(b2r),
                  small(frr), pl.BlockSpec((None, HYENA_HIDDEN, HYENA_ORDER * HYENA_WIDTH), lambda i: (i // half, 0, 0)),
                  small(dl)],
        out_specs=[taps_spec, pl.BlockSpec((HYENA_ORDER, 1, HYENA_WIDTH), lambda i: (0, 0, 0))],
        out_shape=[jax.ShapeDtypeStruct(taps_shape, F32), jax.ShapeDtypeStruct((HYENA_ORDER, 1, HYENA_WIDTH), F32)],
        compiler_params=_params(("arbitrary",)), name="hyena_filter",
    )(z, w1p, b1r, w2, b2r, frr, w3d, dl)


def _dft_mats(n1, n2):
    n = n1 * n2
    a1 = 2 * np.pi * np.outer(np.arange(n1), np.arange(n1)) / n1
    c1, s1 = np.cos(a1), np.sin(a1)
    a2 = 2 * np.pi * np.outer(np.arange(n2), np.arange(n2)) / n2
    c2, s2 = np.cos(a2), np.sin(a2)
    h = n1 // 2
    mats = dict(
        sig=np.block([[c1[:, :h], s1[:, :h]], [-s1[:, :h], c1[:, :h]]]),
        filt=np.concatenate([c1, -s1], axis=0),
        fwd2=np.block([[c2, s2], [-s2, c2]]),
        inv2=np.block([[c2, -s2], [s2, c2]]),
        last=np.block([[c1[:h, :], -s1[:h, :]], [s1[:h, :], c1[:h, :]]]) / n,
    )
    mats = {k: jnp.asarray(v, F32).astype(BF16) for k, v in mats.items()}
    at = (jnp.arange(n1, dtype=jnp.int32)[:, None] * jnp.arange(n2, dtype=jnp.int32)[None, :]).astype(F32) * (2 * math.pi / n)
    tw = tuple(jnp.broadcast_to(f(at)[:, :, None], (n1, n2, LANES)) for f in (jnp.cos, jnp.sin))
    return mats, tw


def _stage1_kernel(m_ref, x_ref, o_ref, *, rows_in, rows_out, complex_in):
    for r in range(SUB):
        sl = pl.ds(r, rows_in, stride=SUB)
        if complex_in:
            x = jnp.concatenate([x_ref[0, sl, :], x_ref[1, sl, :]], axis=0)
        else:
            x = x_ref[sl, :]
        o_ref[pl.ds(r, rows_out, stride=SUB), :] = _dot(m_ref[...], x.astype(BF16))


def _stage1_call(mat, x, col_block, *, complex_in, name):
    w = HYENA_WIDTH
    nl = w // LANES
    if complex_in:
        _, groups, rows8, _ = x.shape
        spec = pl.BlockSpec((2, None, rows8, LANES), lambda j, c: (0, j, 0, col_block * nl + c))
    else:
        groups, rows8, _ = x.shape
        spec = pl.BlockSpec((None, rows8, LANES), lambda j, c: (j, 0, c))
    rows_out = mat.shape[0]
    kern = functools.partial(_stage1_kernel, rows_in=rows8 // SUB, rows_out=rows_out, complex_in=complex_in)
    return pl.pallas_call(
        kern, grid=(groups, nl),
        in_specs=[pl.BlockSpec(mat.shape, lambda j, c: (0, 0)), spec],
        out_specs=pl.BlockSpec((None, rows_out * SUB, LANES), lambda j, c: (j, 0, c)),
        out_shape=jax.ShapeDtypeStruct((groups, rows_out * SUB, w), F32),
        compiler_params=_params(("parallel", "parallel")), name=name,
    )(mat, x)


def _lane_tile(t):
    return jnp.concatenate([t] * (HYENA_WIDTH // LANES), axis=-1)


def _twiddle_dft2(ar, ai, tc, ts, f_ref, n2):
    br = ar * tc + ai * ts
    bi = ai * tc - ar * ts
    x = _dot(f_ref[...], jnp.concatenate([br, bi], axis=0).astype(BF16))
    return x[:n2], x[n2:]


def _spectrum_kernel(f_ref, ar_ref, ai_ref, tc_ref, ts_ref, ss_ref, hr_ref, hi_ref, *, n2):
    tc, ts = _lane_tile(tc_ref[...]), _lane_tile(ts_ref[...])
    w = ar_ref.shape[-1]
    xr, xi = _twiddle_dft2(ar_ref[...].reshape(n2, w), ai_ref[...].reshape(n2, w), tc, ts, f_ref, n2)
    norm = lax.rsqrt(ss_ref[...] + EPS)
    hr_ref[...] = xr * norm
    hi_ref[...] = xi * norm


def _spectrum_call(fwd2, a, twc, tws, ss, order):
    groups, rows8, w = a.shape
    n2, n1 = groups * SUB, rows8 // (2 * SUB)
    col = lambda off: pl.BlockSpec((groups, SUB, w), lambda k: (0, k + off, 0))
    tw = pl.BlockSpec((None, n2, LANES), lambda k: (k, 0, 0))
    out = pl.BlockSpec((None, n2, w), lambda k: (k, 0, 0))
    return pl.pallas_call(
        functools.partial(_spectrum_kernel, n2=n2), grid=(n1,),
        in_specs=[pl.BlockSpec(fwd2.shape, lambda k: (0, 0)), col(0), col(n1), tw, tw,
                  pl.BlockSpec((None, 1, w), lambda k: (order, 0, 0))],
        out_specs=[out, out],
        out_shape=[jax.ShapeDtypeStruct((n1, n2, w), F32)] * 2,
        compiler_params=_params(("parallel",)), name="hyena_filter_spectrum",
    )(fwd2, a, a, twc, tws, ss)


def _mid_kernel(f_ref, g_ref, ar_ref, ai_ref, tc_ref, ts_ref, hr_ref, hi_ref, or_ref, oi_ref, *, n2):
    tc, ts = _lane_tile(tc_ref[...]), _lane_tile(ts_ref[...])
    w = ar_ref.shape[-1]
    xr, xi = _twiddle_dft2(ar_ref[...].reshape(n2, w), ai_ref[...].reshape(n2, w), tc, ts, f_ref, n2)
    hr, hi = hr_ref[...], hi_ref[...]
    yr = xr * hr - xi * hi
    yi = xr * hi + xi * hr
    y = _dot(g_ref[...], jnp.concatenate([yr, yi], axis=0).astype(BF16))
    cr, ci = y[:n2], y[n2:]
    or_ref[...] = (cr * tc - ci * ts).reshape(or_ref.shape)
    oi_ref[...] = (ci * tc + cr * ts).reshape(oi_ref.shape)


def _mid_call(fwd2, inv2, a, twc, tws, hr, hi):
    groups, rows8, w = a.shape
    n2, n1 = groups * SUB, rows8 // (2 * SUB)
    col = lambda off: pl.BlockSpec((groups, SUB, w), lambda k: (0, k + off, 0))
    tw = pl.BlockSpec((None, n2, LANES), lambda k: (k, 0, 0))
    hs = pl.BlockSpec((None, n2, w), lambda k: (k, 0, 0))
    mat = pl.BlockSpec(fwd2.shape, lambda k: (0, 0))
    return pl.pallas_call(
        functools.partial(_mid_kernel, n2=n2), grid=(n1,),
        in_specs=[mat, mat, col(0), col(n1), tw, tw, hs, hs],
        out_specs=[col(0)] * 2,
        out_shape=[jax.ShapeDtypeStruct((groups, n1 * SUB, w), F32)] * 2,
        compiler_params=_params(("parallel",)), name="hyena_conv_mid",
    )(fwd2, inv2, a, a, twc, tws, hr, hi)


def _last_kernel(m_ref, br_ref, bi_ref, z_ref, x_ref, skip_ref, o_ref, *, n1):
    skip = skip_ref[...]
    half = n1 // 2
    for r in range(SUB):
        full, part = pl.ds(r, n1, stride=SUB), pl.ds(r, half, stride=SUB)
        y = _dot(m_ref[...], jnp.concatenate([br_ref[full, :], bi_ref[full, :]], axis=0).astype(BF16))
        o_ref[0, part, :] = x_ref[0, part, :] * (y[:half] + z_ref[0, part, :] * skip)
        o_ref[1, part, :] = x_ref[1, part, :] * (y[half:] + z_ref[1, part, :] * skip)


def _last_call(mat, br, bi, zsrc, zcol, xsrc, xcol, skip, order):
    groups, rows8, w = br.shape
    n1 = rows8 // SUB
    nl = w // LANES
    b_spec = pl.BlockSpec((None, rows8, LANES), lambda j, c: (j, 0, c))
    seq = lambda col: pl.BlockSpec((2, None, rows8 // 2, LANES), lambda j, c: (0, j, 0, col * nl + c))
    return pl.pallas_call(
        functools.partial(_last_kernel, n1=n1), grid=(groups, nl),
        in_specs=[pl.BlockSpec(mat.shape, lambda j, c: (0, 0)), b_spec, b_spec, seq(zcol), seq(xcol),
                  pl.BlockSpec((None, 1, LANES), lambda j, c: (order, 0, c))],
        out_specs=seq(0),
        out_shape=jax.ShapeDtypeStruct((2, groups, rows8 // 2, w), F32),
        compiler_params=_params(("parallel", "parallel")), name="hyena_conv_last",
    )(mat, br, bi, zsrc, xsrc, skip)


def _hyena_long(uct, taps_t, ss, skip):
    n2 = uct.shape[1] * SUB
    mats, (twc, tws) = _dft_mats(FFT_N1, n2)
    skip3 = skip.reshape(HYENA_ORDER, 1, HYENA_WIDTH)
    z, zcol = uct, 0
    for o in range(HYENA_ORDER):
        af = _stage1_call(mats["filt"], taps_t[o], 0, complex_in=False, name="hyena_filter_dft1")
        hr, hi = _spectrum_call(mats["fwd2"], af, twc, tws, ss, o)
        a = _stage1_call(mats["sig"], z, zcol, complex_in=True, name="hyena_conv_first")
        br, bi = _mid_call(mats["fwd2"], mats["inv2"], a, twc, tws, hr, hi)
        z = _last_call(mats["last"], br, bi, z, zcol, uct, o + 1, skip3, o)
        zcol = 0
    return z


def _ctx_hyena_kernel(v_ref, x1_ref, x2_ref, taps_ref, ss_ref, skip_ref, fh_ref, fz_ref, fi_ref, o_ref, *, n):
    nn = 2 * n
    stack = lambda r: jnp.concatenate([r[0], r[1]], axis=0)
    z = stack(v_ref)
    for o, x_ref in enumerate((x1_ref, x2_ref)):
        h = _dot(fh_ref[...], taps_ref[o], HIGHEST) * lax.rsqrt(ss_ref[o] + EPS)
        hr, hi = h[:nn], h[nn:]
        zf = _dot(fz_ref[...], z, HIGHEST)
        zr, zi = zf[:nn], zf[nn:]
        y = _dot(fi_ref[...], jnp.concatenate([zr * hr - zi * hi, zr * hi + zi * hr], axis=0), HIGHEST)
        z = stack(x_ref) * (y + z * skip_ref[o])
    o_ref[0] = z[:n].astype(o_ref.dtype)
    o_ref[1] = z[n:].astype(o_ref.dtype)


def _ctx_hyena_call(u, taps, ss, skip):
    _, n, _ = u.shape
    nn = 2 * n
    ang = 2 * np.pi * np.outer(np.arange(nn), np.arange(nn)) / nn
    c, s = np.cos(ang), np.sin(ang)
    fh = jnp.asarray(np.concatenate([c, -s], axis=0), F32)
    fz = jnp.asarray(np.block([[c[:, :n], s[:, :n]], [-s[:, :n], c[:, :n]]]), F32)
    fi = jnp.asarray(np.block([[c[:n, :], -s[:n, :]], [s[:n, :], c[:n, :]]]) / nn, F32)
    nchunk = HYENA_WIDTH // LANES
    skip3 = skip.reshape(HYENA_ORDER, 1, HYENA_WIDTH)
    small = lambda a: pl.BlockSpec(a.shape, lambda j: (0,) * a.ndim)
    part = lambda g: pl.BlockSpec((2, n, LANES), lambda j, g=g: (0, 0, g * nchunk + j))

    return pl.pallas_call(
        functools.partial(_ctx_hyena_kernel, n=n), grid=(nchunk,),
        in_specs=[part(0), part(1), part(2),
                  pl.BlockSpec((HYENA_ORDER, nn, LANES), lambda j: (0, 0, j)),
                  pl.BlockSpec((HYENA_ORDER, 1, LANES), lambda j: (0, 0, j)),
                  pl.BlockSpec((HYENA_ORDER, 1, LANES), lambda j: (0, 0, j)),
                  small(fh), small(fz), small(fi)],
        out_specs=pl.BlockSpec((2, n, LANES), lambda j: (0, 0, j)),
        out_shape=jax.ShapeDtypeStruct((2, n, HYENA_WIDTH), BF16),
        compiler_params=_params(("parallel",)), name="hyena_ctx",
    )(u, u, u, taps, ss, skip3, fh, fz, fi)


def _outproj_kernel(x_ref, ya_ref, yb_ref, g_ref, w_ref, gate_ref, o_ref):
    tm, half = ya_ref.shape
    ya = (ya_ref[...].astype(F32) * g_ref[:, :half].astype(F32)).astype(BF16)
    yb = (yb_ref[...].reshape(tm, half).astype(F32) * g_ref[:, half:].astype(F32)).astype(BF16)
    upd = _dot(ya, w_ref[:half, :]) + _dot(yb, w_ref[half:, :])
    o_ref[...] = x_ref[...] + gate_ref[...] * upd


def _outproj_call(x2d, ya, yb, gates, w_bf, gate_vec, *, seq_len, tm, per_batch_mod, yb_t_layout=False):
    r, d = x2d.shape
    tps = seq_len // tm
    half = ya.shape[-1]
    if yb_t_layout:
        yb_spec = pl.BlockSpec((None, tm // SUB, SUB, half), lambda i: (i // tps, 0, i % tps, 0))
    else:
        yb_spec = pl.BlockSpec((tm, half), lambda i: (i, 0))
    mod_map = (lambda i: (i // tps, 0, 0)) if per_batch_mod else (lambda i: (0, 0, 0))
    return pl.pallas_call(
        _outproj_kernel, grid=(r // tm,),
        in_specs=[pl.BlockSpec((tm, d), lambda i: (i, 0)), pl.BlockSpec((tm, half), lambda i: (i, 0)), yb_spec,
                  pl.BlockSpec((tm, 2 * half), lambda i: (i, 0)), pl.BlockSpec(w_bf.shape, lambda i: (0, 0)),
                  pl.BlockSpec((None, 1, d), mod_map)],
        out_specs=pl.BlockSpec((tm, d), lambda i: (i, 0)),
        out_shape=jax.ShapeDtypeStruct((r, d), F32),
        compiler_params=_params(("parallel",)), name="outproj",
    )(x2d, ya, yb, gates, w_bf, gate_vec)


def _rope_tables(n_tokens):
    t = jnp.arange(n_tokens, dtype=jnp.int32)
    row = (t // GRID_W).astype(F32)
    col = (t % GRID_W).astype(F32)
    n_pairs = HEAD_DIM // 4
    inv = ROPE_THETA ** (-jnp.arange(n_pairs, dtype=F32) / n_pairs)
    ang = jnp.concatenate([row[:, None] * inv, col[:, None] * inv], axis=-1)
    cos = jnp.repeat(jnp.cos(ang), 2, axis=-1)
    sin = jnp.repeat(jnp.sin(ang), 2, axis=-1)
    even = (jnp.arange(HEAD_DIM) % 2 == 0)[None, :]
    tile = lambda a: jnp.tile(a, (1, LANES // HEAD_DIM))
    return tile(cos), tile(jnp.where(even, -sin, 0.0)), tile(jnp.where(even, 0.0, sin))


def _gain_rows(*gains):
    rows = [jnp.tile(g, LANES // HEAD_DIM) for g in gains]
    rows += [jnp.zeros((LANES,), F32)] * (8 - len(rows))
    return jnp.stack(rows)


def kernel(x, c, ctx, c_ctx, norm_g, w_ada, b_ada, w_in, w_out, glob_q_norm, glob_k_norm, hy_conv_w, hy_conv_b, hy_w1, hy_b1, hy_w2, hy_b2, hy_freq, hy_w3, hy_skip, win_q_norm, win_k_norm, win_sink, nat_q_norm, nat_k_norm, nat_rpb):
    b, s, d = x.shape
    n_ctx = ctx.shape[1]
    n2 = 2 * s // FFT_N1
    tm = n2
    w_in_bf, w_out_bf = w_in.astype(BF16), w_out.astype(BF16)

    cond = jnp.zeros((8, d), F32).at[:b].set(c).at[b].set(c_ctx)
    mod = _ada_call(cond, w_ada, b_ada)
    shift, scale, gate = (mod[:, :, i * d:(i + 1) * d] for i in range(3))
    lat = lambda m, l: m[l, :b].reshape(b, 1, d)
    cx = lambda m, l: m[l, b:b + 1].reshape(1, 1, d)

    rope = _rope_tables(s)
    x2 = x.reshape(b * s, d)
    xc2 = ctx.reshape(b * n_ctx, d)
    heads4 = lambda a, n: a.reshape(a.shape[0], b, n, a.shape[-1])

    gains0 = _gain_rows(glob_q_norm[0], glob_k_norm[0])
    gw, kw = GROUP_WIDTH, N_KV_HEADS * HEAD_DIM
    hyw = (HYENA_ORDER + 1) * HYENA_WIDTH
    qs2 = QK_SCALE * math.log2(math.e)
    plan_lat = (("qk", 0, gw, 0, True, qs2), ("qk", gw, kw, 1, True, 1.0), ("vones", gw + kw, kw),
                ("hy", gw + 2 * kw, hyw), ("gate", gw + 2 * kw + hyw, 2 * gw))
    plan_ctx = (("qk", 0, gw, 0, False, qs2), ("qk", gw, kw, 1, False, 1.0), ("vones", gw + kw, kw),
                ("hy", gw + 2 * kw, hyw), ("gate", gw + 2 * kw + hyw, 2 * gw))
    conv = (hy_conv_w[0], hy_conv_b[0])
    q0, k0, v0, uct, g0 = _inproj_call(
        x2, lat(shift, 0), lat(scale, 0), norm_g[0], w_in_bf[0], plan_lat, seq_len=s, tm=tm, per_batch_mod=True,
        rope_tabs=rope, gains=gains0, conv=conv, hy_t_layout=True, name="inproj_even")
    qc, kc, vc, ucc, gc = _inproj_call(
        xc2, cx(shift, 0), cx(scale, 0), norm_g[0], w_in_bf[0], plan_ctx, seq_len=n_ctx, tm=n_ctx,
        per_batch_mod=False, gains=gains0, conv=conv, name="inproj_even_ctx")
    kc4, vc4 = heads4(kc, n_ctx), heads4(vc, n_ctx)
    ya = _flash_call(heads4(q0, s), kc4, vc4, heads4(k0, s), heads4(v0, s), tq=256, name="global_gqa")
    yca = _flash_call(heads4(qc, n_ctx), kc4, vc4, tq=n_ctx, name="ctx_gqa")

    filt = (hy_w1[0], hy_b1[0], hy_w2[0], hy_b2[0], hy_freq[0], hy_w3[0])
    taps_t, ss = _filter_call(s, *filt, tm=n2, t_layout=True)
    ybt = _hyena_long(uct, taps_t, ss, hy_skip[0])
    taps_c, ss_c = _filter_call(n_ctx, *filt, tm=n_ctx, t_layout=False)
    ycb = _ctx_hyena_call(ucc.reshape(b, n_ctx, hyw), taps_c, ss_c, hy_skip[0])

    x2 = _outproj_call(x2, ya.reshape(b * s, gw), ybt, g0, w_out_bf[0], lat(gate, 0), seq_len=s, tm=tm,
                       per_batch_mod=True, yb_t_layout=True)
    xc2 = _outproj_call(xc2, yca.reshape(b * n_ctx, gw), ycb.reshape(b * n_ctx, gw), gc, w_out_bf[0], cx(gate, 0),
                        seq_len=n_ctx, tm=n_ctx, per_batch_mod=False)

    gains1 = _gain_rows(win_q_norm[0], win_k_norm[0], nat_q_norm[0], nat_k_norm[0])
    o_nq = gw + 2 * kw
    plan_lat = (("qk", 0, gw, 0, True, QK_SCALE), ("qk", gw, kw, 1, True, 1.0), ("v", gw + kw, kw),
                ("qk", o_nq, gw, 2, False, QK_SCALE), ("qk", o_nq + gw, gw, 3, False, 1.0), ("v", o_nq + 2 * gw, gw),
                ("gate", o_nq + 3 * gw, 2 * gw))
    plan_ctx = (("qk", gw, kw, 1, False, 1.0), ("v", gw + kw, kw),
                ("qk", o_nq + gw, gw, 3, False, 1.0), ("v", o_nq + 2 * gw, gw))
    qw, kwl, vwl, qn, kn, vn, g1 = _inproj_call(
        x2, lat(shift, 1), lat(scale, 1), norm_g[1], w_in_bf[1], plan_lat, seq_len=s, tm=tm, per_batch_mod=True,
        rope_tabs=rope, gains=gains1, name="inproj_odd")
    kwc, vwc, knc, vnc = _inproj_call(
        xc2, cx(shift, 1), cx(scale, 1), norm_g[1], w_in_bf[1], plan_ctx, seq_len=n_ctx, tm=n_ctx,
        per_batch_mod=False, gains=gains1, name="inproj_odd_ctx")
    yw = _window_call(win_sink[0], heads4(qw, s), heads4(kwc, n_ctx), heads4(vwc, n_ctx), heads4(kwl, s), heads4(vwl, s))
    table = _rpb_table_call(nat_rpb[0])
    yn = _na_call(heads4(qn, s), heads4(kn, s), heads4(vn, s), heads4(knc, n_ctx), heads4(vnc, n_ctx), table)
    x2 = _outproj_call(x2, yw.reshape(b * s, gw), yn.reshape(b * s, gw), g1, w_out_bf[1], lat(gate, 1), seq_len=s,
                       tm=tm, per_batch_mod=True)
    return x2.reshape(b, s, d)
```

```python
import functools
import math

import numpy as np
import jax
import jax.numpy as jnp
from jax import lax
from jax.experimental import pallas as pl
from jax.experimental.pallas import tpu as pltpu

F32 = jnp.float32
BF16 = jnp.bfloat16
HIGHEST = lax.Precision.HIGHEST

GRID_W = 64
HEAD_DIM = 64
GROUP_WIDTH = 512
N_Q_HEADS = 8
N_KV_HEADS = 2
GQA_GROUP = N_Q_HEADS // N_KV_HEADS
HYENA_WIDTH = 512
HYENA_ORDER = 2
HYENA_EMB = 33
HYENA_HIDDEN = 64
HYENA_DECAY_TARGET = 1e-2
HYENA_SHORT_DECAY_PCT = 0.3
HYENA_LONG_DECAY_PCT = 1.5
WINDOW = 128
NA_ROWS = 8
NA_COLS = 16
ROPE_THETA = 10000.0
EPS = 1e-6
NEG_INF = -1e30
LOG2E = math.log2(math.e)
QK_SCALE2 = HEAD_DIM ** -0.5 * LOG2E

LANES = 128
SUB = 8
HALO = 16
FFT_N1 = 128
FLASH_UNROLL = 4
VMEM_LIMIT = 56 * 1024 * 1024


def _dot(a, b, precision=None):
    return jnp.dot(a, b, preferred_element_type=F32, precision=precision)


def _dot_nt(a, b):
    return lax.dot_general(a, b, (((1,), (1,)), ((), ())), preferred_element_type=F32)


def _params(sem):
    return pltpu.CompilerParams(dimension_semantics=sem, vmem_limit_bytes=VMEM_LIMIT)


def _silu(v):
    return v / (1.0 + jnp.exp(-v))


def _softmax2_pv(s, v, sink=None):
    m = jnp.max(s, axis=1, keepdims=True)
    if sink is not None:
        m = jnp.maximum(m, sink)
    p = jnp.exp2(s - m)
    l = jnp.sum(p, axis=1, keepdims=True)
    if sink is not None:
        l = l + jnp.exp2(sink - m)
    return _dot(p.astype(BF16), v) / l


def _ada_kernel(cond_ref, w_ref, b_ref, o_ref):
    o_ref[0] = _dot(_silu(cond_ref[...]), w_ref[0], HIGHEST) + b_ref[0]


def _ada_call(cond, w_ada, b_ada):
    depth, d, d3 = w_ada.shape
    tn = 1024
    return pl.pallas_call(
        _ada_kernel,
        grid=(depth, d3 // tn),
        in_specs=[pl.BlockSpec((8, d), lambda l, j: (0, 0)),
                  pl.BlockSpec((1, d, tn), lambda l, j: (l, 0, j)),
                  pl.BlockSpec((1, 1, tn), lambda l, j: (l, 0, j))],
        out_specs=pl.BlockSpec((1, 8, tn), lambda l, j: (l, 0, j)),
        out_shape=jax.ShapeDtypeStruct((depth, 8, d3), F32),
        compiler_params=_params(("parallel", "parallel")),
        name="adaln",
    )(cond, w_ada, b_ada.reshape(depth, 1, d3))


def _inproj_kernel(*refs, plan, tm, tps, has_halo, has_rope):
    it = iter(refs)
    x_ref = next(it)
    if has_halo:
        xp_ref, xn_ref = next(it), next(it)
    sh_ref, sc_ref, g_ref, w_ref = next(it), next(it), next(it), next(it)
    if has_rope:
        cos_ref, s1_ref, s2_ref = next(it), next(it), next(it)
    gains_ref, mavg_ref = next(it), next(it)
    if has_halo:
        cw_ref, cb_ref = next(it), next(it)
    outs = [next(it) for _ in plan]
    if has_halo:
        xe_ref, hu_ref = next(it), next(it)

    g, sh, sc = g_ref[...], sh_ref[...], sc_ref[...]

    def modnorm(xv):
        ms = jnp.mean(xv * xv, axis=-1, keepdims=True)
        return (xv * lax.rsqrt(ms + EPS) * g) * (1.0 + sc) + sh

    h = modnorm(x_ref[...]).astype(BF16)
    mavg = mavg_ref[...]

    for seg, o_ref in zip(plan, outs):
        kind, start, width = seg[0], seg[1], seg[2]
        if kind == "hy":
            xe_ref[0:HALO] = modnorm(xp_ref[...]).astype(BF16)
            xe_ref[HALO:HALO + tm] = h
            xe_ref[HALO + tm:HALO + tm + HALO] = modnorm(xn_ref[...]).astype(BF16)
            hu_ref[...] = _dot(xe_ref[...], w_ref[:, start:start + width])
            pos = pl.program_id(0) % tps
            row = lax.broadcasted_iota(jnp.int32, (tm, 1), 0)
            kill_prev = jnp.where(pos == 0, 0, -1)
            kill_next = jnp.where(pos == tps - 1, tm - 1, -1)
            a = jnp.where(row == kill_prev, 0.0, hu_ref[pl.ds(HALO - 1, tm), :])
            b = hu_ref[pl.ds(HALO, tm), :]
            c = jnp.where(row == kill_next, 0.0, hu_ref[pl.ds(HALO + 1, tm), :])
            uc = a * cw_ref[0:1, :] + b * cw_ref[1:2, :] + c * cw_ref[2:3, :] + cb_ref[...]
            o_ref[...] = uc.reshape(o_ref.shape)
            continue
        pv = _dot(h, w_ref[:, start:start + width])
        if kind == "gate":
            o_ref[...] = _silu(pv).astype(o_ref.dtype)
        elif kind == "v":
            for hd in range(width // HEAD_DIM):
                o_ref[hd] = pv[:, hd * HEAD_DIM:(hd + 1) * HEAD_DIM].astype(o_ref.dtype)
        elif kind == "vones":
            lane = lax.broadcasted_iota(jnp.int32, (tm, LANES - HEAD_DIM), 1)
            tail = jnp.where(lane == 0, 1.0, 0.0)
            for hd in range(width // HEAD_DIM):
                o_ref[hd] = jnp.concatenate([pv[:, hd * HEAD_DIM:(hd + 1) * HEAD_DIM], tail], axis=1).astype(o_ref.dtype)
        else:
            gain_row, rope, scale = seg[3], seg[4], seg[5]
            gain = gains_ref[gain_row:gain_row + 1, :]
            for c in range(width // LANES):
                v = pv[:, c * LANES:(c + 1) * LANES]
                sq = v * v
                hi = sq.astype(BF16)
                lo = (sq - hi.astype(F32)).astype(BF16)
                ms = _dot(hi, mavg) + _dot(lo, mavg)
                vn = v * lax.rsqrt(ms + EPS) * gain
                if rope:
                    vn = (vn * cos_ref[...] + pltpu.roll(vn, LANES - 1, 1) * s1_ref[...]
                          + pltpu.roll(vn, 1, 1) * s2_ref[...])
                vn = (vn * scale).astype(o_ref.dtype)
                o_ref[2 * c] = vn[:, :HEAD_DIM]
                o_ref[2 * c + 1] = vn[:, HEAD_DIM:]


def _inproj_call(x2d, shift, scale, norm_g, w_bf, plan, *, seq_len, tm, per_batch_mod,
                 rope_tabs=None, gains=None, conv=None, hy_t_layout=False, name="inproj"):
    r, d = x2d.shape
    tps = seq_len // tm
    n_tiles = r // tm
    n_seq = r // seq_len
    has_halo = any(s[0] == "hy" for s in plan)
    has_rope = rope_tabs is not None
    hb = tm // HALO
    mod_map = (lambda i: (i // tps, 0, 0)) if per_batch_mod else (lambda i: (0, 0, 0))

    args, specs = [x2d], [pl.BlockSpec((tm, d), lambda i: (i, 0))]
    if has_halo:
        args += [x2d, x2d]
        specs += [pl.BlockSpec((HALO, d), lambda i: (jnp.maximum(i * hb - 1, 0), 0)),
                  pl.BlockSpec((HALO, d), lambda i: (jnp.minimum((i + 1) * hb, r // HALO - 1), 0))]
    args += [shift, scale, norm_g.reshape(1, d), w_bf]
    specs += [pl.BlockSpec((None, 1, d), mod_map), pl.BlockSpec((None, 1, d), mod_map),
              pl.BlockSpec((1, d), lambda i: (0, 0)),
              pl.BlockSpec(w_bf.shape, lambda i: (0, 0))]
    if has_rope:
        args += list(rope_tabs)
        specs += [pl.BlockSpec((tm, LANES), lambda i: (i % tps, 0))] * 3
    mavg = np.kron(np.eye(LANES // HEAD_DIM), np.full((HEAD_DIM, HEAD_DIM), 1.0 / HEAD_DIM))
    args += [gains, jnp.asarray(mavg, BF16)]
    specs += [pl.BlockSpec(gains.shape, lambda i: (0, 0)), pl.BlockSpec((LANES, LANES), lambda i: (0, 0))]
    if has_halo:
        cw, cb = conv
        args += [cw, cb.reshape(1, -1)]
        specs += [pl.BlockSpec(cw.shape, lambda i: (0, 0)), pl.BlockSpec((1, cb.shape[-1]), lambda i: (0, 0))]

    out_shapes, out_specs = [], []
    for seg in plan:
        kind, width = seg[0], seg[2]
        if kind == "hy":
            if hy_t_layout:
                out_shapes.append(jax.ShapeDtypeStruct((n_seq, tm // SUB, tps * SUB, width), F32))
                out_specs.append(pl.BlockSpec((None, tm // SUB, SUB, width), lambda i: (i // tps, 0, i % tps, 0)))
            else:
                out_shapes.append(jax.ShapeDtypeStruct((r, width), F32))
                out_specs.append(pl.BlockSpec((tm, width), lambda i: (i, 0)))
        elif kind == "gate":
            out_shapes.append(jax.ShapeDtypeStruct((r, width), BF16))
            out_specs.append(pl.BlockSpec((tm, width), lambda i: (i, 0)))
        else:
            nh = width // HEAD_DIM
            hw = LANES if kind == "vones" else HEAD_DIM
            out_shapes.append(jax.ShapeDtypeStruct((nh, r, hw), BF16))
            out_specs.append(pl.BlockSpec((nh, tm, hw), lambda i: (0, i, 0)))
    scratch = []
    if has_halo:
        hw = [s[2] for s in plan if s[0] == "hy"][0]
        scratch = [pltpu.VMEM((tm + 2 * HALO, d), BF16), pltpu.VMEM((tm + 2 * HALO, hw), F32)]
    kern = functools.partial(_inproj_kernel, plan=plan, tm=tm, tps=tps, has_halo=has_halo, has_rope=has_rope)
    return pl.pallas_call(
        kern, grid=(n_tiles,), in_specs=specs, out_specs=out_specs, out_shape=out_shapes,
        scratch_shapes=scratch, compiler_params=_params(("parallel",)), name=name,
    )(*args)


def _flash_kernel(q_ref, kc_ref, vc_ref, *rest, tq, tk, n_lat_chunks, unroll):
    if n_lat_chunks:
        kl_ref, vl_ref, o_ref, m_ref, acc_ref, s_ref = rest
    else:
        o_ref, m_ref, acc_ref = rest
    rows = GQA_GROUP * tq
    q = q_ref[...].reshape(rows, HEAD_DIM)

    def softmax_pv(s, vb, first):
        smax = jnp.max(s, axis=1, keepdims=True)
        if first:
            m_new = jnp.broadcast_to(smax, (rows, LANES))
        else:
            m_prev = m_ref[...]
            m_new = jnp.maximum(m_prev, smax)
        m_ref[...] = m_new
        p = jnp.exp2(s - jnp.concatenate([m_new] * (s.shape[1] // LANES), axis=1)).astype(BF16)
        pv = _dot(p, vb)
        if first:
            acc_ref[...] = pv
        else:
            acc_ref[...] = jnp.exp2(m_prev - m_new) * acc_ref[...] + pv

    softmax_pv(_dot_nt(q, kc_ref[...]), vc_ref[...], True)
    if n_lat_chunks:
        s_ref[0] = _dot_nt(q, kl_ref[pl.ds(0, tk), :])

        def half(j, cur):
            nxt = pl.multiple_of(jnp.minimum(j + 1, n_lat_chunks - 1) * tk, tk)
            s_ref[1 - cur] = _dot_nt(q, kl_ref[pl.ds(nxt, tk), :])
            softmax_pv(s_ref[cur], vl_ref[pl.ds(pl.multiple_of(j * tk, tk), tk), :], False)

        def body(i, carry):
            for u in range(unroll):
                half(unroll * i + u, u % 2)
            return carry
        lax.fori_loop(0, n_lat_chunks // unroll, body, 0)
    acc = acc_ref[...]
    o = acc[:, :HEAD_DIM] / acc[:, HEAD_DIM:HEAD_DIM + 1]
    o_ref[...] = jnp.concatenate([o[gi * tq:(gi + 1) * tq] for gi in range(GQA_GROUP)], axis=-1).astype(o_ref.dtype)


def _flash_call(q, kc, vc, kl=None, vl=None, *, tq, tk=1024, name="gqa"):
    _, b, sq, _ = q.shape
    c = kc.shape[2]
    kv_spec = lambda n, w: pl.BlockSpec((None, None, n, w), lambda bi, j, i: (j, bi, 0, 0))
    args = [q, kc, vc]
    specs = [pl.BlockSpec((GQA_GROUP, None, tq, HEAD_DIM), lambda bi, j, i: (j, bi, i, 0)),
             kv_spec(c, HEAD_DIM), kv_spec(c, LANES)]
    rows = GQA_GROUP * tq
    scratch = [pltpu.VMEM((rows, LANES), F32), pltpu.VMEM((rows, LANES), F32)]
    n_lat = 0
    if kl is not None:
        s = kl.shape[2]
        n_lat = s // tk
        unroll = math.gcd(FLASH_UNROLL, n_lat)
        assert unroll % 2 == 0
        args += [kl, vl]
        specs += [kv_spec(s, HEAD_DIM), kv_spec(s, LANES)]
        scratch.append(pltpu.VMEM((2, rows, tk), F32))
    kern = functools.partial(_flash_kernel, tq=tq, tk=tk, n_lat_chunks=n_lat, unroll=unroll if n_lat else 0)
    return pl.pallas_call(
        kern, grid=(b, N_KV_HEADS, sq // tq), in_specs=specs,
        out_specs=pl.BlockSpec((None, tq, GQA_GROUP * HEAD_DIM), lambda bi, j, i: (bi, i, j)),
        out_shape=jax.ShapeDtypeStruct((b, sq, GROUP_WIDTH), BF16),
        scratch_shapes=scratch,
        compiler_params=_params(("parallel", "parallel", "arbitrary")), name=name,
    )(*args)


def _head_pipeline(n_heads, scores_fn, finish_fn, s_ref):
    s_ref[0] = scores_fn(0)

    def half(h, cur):
        s_ref[1 - cur] = scores_fn(jnp.minimum(h + 1, n_heads - 1))
        finish_fn(h, s_ref[cur])

    def body(i, carry):
        half(2 * i, 0)
        half(2 * i + 1, 1)
        return carry
    lax.fori_loop(0, n_heads // 2, body, 0)


def _window_kernel(sink_ref, q_ref, kc_ref, vc_ref, kl_ref, vl_ref, o_ref, s_ref, mask_ref, oh_ref, *, tq, span, seq):
    j, i = pl.program_id(1), pl.program_id(2)
    start = pl.multiple_of(jnp.clip(i * tq - WINDOW, 0, seq - span), LANES)
    n_ctx = kc_ref.shape[0]
    qpos = i * tq + lax.broadcasted_iota(jnp.int32, (tq, span), 0)
    kpos = start + lax.broadcasted_iota(jnp.int32, (tq, span), 1)
    mask_ref[...] = jnp.concatenate([jnp.where(jnp.abs(qpos - kpos) <= WINDOW, 0.0, NEG_INF),
                                     jnp.zeros((tq, n_ctx), F32)], axis=1)

    def scores(g):
        k_all = jnp.concatenate([kl_ref[pl.ds(start, span), :], kc_ref[...]], axis=0)
        return _dot_nt(q_ref[g], k_all) + mask_ref[...]

    def finish(g, s):
        v_all = jnp.concatenate([vl_ref[pl.ds(start, span), :], vc_ref[...]], axis=0)
        oh_ref[g] = _softmax2_pv(s, v_all, sink_ref[j * GQA_GROUP + g] * LOG2E)

    _head_pipeline(GQA_GROUP, scores, finish, s_ref)
    o_ref[...] = jnp.concatenate([oh_ref[g] for g in range(GQA_GROUP)], axis=-1).astype(o_ref.dtype)


def _window_call(sink, q, kc, vc, kl, vl, *, tq=256):
    _, b, s, _ = q.shape
    c = kc.shape[2]
    span = tq + 2 * WINDOW
    kern = functools.partial(_window_kernel, tq=tq, span=span, seq=s)
    full = lambda n: pl.BlockSpec((None, None, n, HEAD_DIM), lambda bi, j, i: (j, bi, 0, 0))
    return pl.pallas_call(
        kern, grid=(b, N_KV_HEADS, s // tq),
        in_specs=[pl.BlockSpec(memory_space=pltpu.SMEM),
                  pl.BlockSpec((GQA_GROUP, None, tq, HEAD_DIM), lambda bi, j, i: (j, bi, i, 0)),
                  full(c), full(c), full(s), full(s)],
        out_specs=pl.BlockSpec((None, tq, GQA_GROUP * HEAD_DIM), lambda bi, j, i: (bi, i, j)),
        out_shape=jax.ShapeDtypeStruct((b, s, GROUP_WIDTH), BF16),
        scratch_shapes=[pltpu.VMEM((2, tq, span + c), F32), pltpu.VMEM((tq, span + c), F32),
                        pltpu.VMEM((GQA_GROUP, tq, HEAD_DIM), F32)],
        compiler_params=_params(("parallel", "parallel", "arbitrary")), name="window_gqa",
    )(sink, q, kc, vc, kl, vl)


NA_BLOCK = 2 * GRID_W
NA_Q_TILES = 2
NA_KEY_BLOCKS = NA_Q_TILES + NA_ROWS // 2
NA_SHIFT0 = NA_ROWS + 2 * (NA_Q_TILES - 1)
NA_SHIFTS = NA_SHIFT0 // 2 + NA_KEY_BLOCKS


def _rpb_table_kernel(rpb_ref, o_ref):
    h, ei = pl.program_id(0), pl.program_id(1)
    e = 2 * ei - NA_SHIFT0
    sub = lax.broadcasted_iota(jnp.int32, (NA_BLOCK, NA_BLOCK), 0)
    lane = lax.broadcasted_iota(jnp.int32, (NA_BLOCK, NA_BLOCK), 1)
    qc, kc = sub % GRID_W, lane % GRID_W
    dr = e + lane // GRID_W - sub // GRID_W
    dc = kc - qc
    c0 = jnp.clip(qc - NA_COLS // 2, 0, GRID_W - NA_COLS)
    n_dc = 2 * NA_COLS - 1
    per_head = (2 * NA_ROWS - 1) * n_dc
    val = jnp.zeros((NA_BLOCK, NA_BLOCK), F32)
    for ddr in (-1, 0, 1):
        d = e + ddr
        base = h * per_head + (jnp.clip(d, 1 - NA_ROWS, NA_ROWS - 1) + NA_ROWS - 1) * n_dc

        def body(c, v, d=d, base=base):
            return jnp.where(jnp.where(dr == d, dc, n_dc) == c - (NA_COLS - 1), rpb_ref[base + c] * LOG2E, v)
        val = lax.fori_loop(0, n_dc, body, val)
    ok = jnp.where(kc >= c0, jnp.where(kc < c0 + NA_COLS, jnp.abs(dr), NA_ROWS), NA_ROWS) < NA_ROWS
    o_ref[0, 0] = jnp.where(ok, val, NEG_INF)


def _rpb_table_call(rpb):
    h = rpb.shape[0]
    return pl.pallas_call(
        _rpb_table_kernel, grid=(h, NA_SHIFTS),
        in_specs=[pl.BlockSpec(memory_space=pltpu.SMEM)],
        out_specs=pl.BlockSpec((1, 1, NA_BLOCK, NA_BLOCK), lambda hh, e: (hh, e, 0, 0)),
        out_shape=jax.ShapeDtypeStruct((h, NA_SHIFTS, NA_BLOCK, NA_BLOCK), F32),
        compiler_params=_params(("parallel", "parallel")), name="rpb_table",
    )(rpb.reshape(-1))


def _na_first_key_row(i, rows):
    return jnp.clip(2 * NA_Q_TILES * i - NA_ROWS // 2, 0, rows - 2 * NA_KEY_BLOCKS)


def _na_kernel(*refs, rows):
    q_ref = refs[0]
    k_refs = refs[1:1 + NA_KEY_BLOCKS]
    v_refs = refs[1 + NA_KEY_BLOCKS:1 + 2 * NA_KEY_BLOCKS]
    kc_ref, vc_ref, tab_ref, o_ref, s_ref, mask_ref, oh_ref = refs[1 + 2 * NA_KEY_BLOCKS:]
    i = pl.program_id(1)
    rs = _na_first_key_row(i, rows)
    row0 = 2 * NA_Q_TILES * i
    sub = lax.broadcasted_iota(jnp.int32, (NA_BLOCK, NA_BLOCK), 0)
    lane = lax.broadcasted_iota(jnp.int32, (NA_BLOCK, NA_BLOCK), 1)
    shift_idx = {}
    for ta in range(NA_Q_TILES):
        r0 = jnp.clip(row0 + 2 * ta + sub // GRID_W - NA_ROWS // 2, 0, rows - NA_ROWS)
        for tb in range(NA_KEY_BLOCKS):
            off = rs + 2 * tb + lane // GRID_W - r0
            mask_ref[ta, tb] = jnp.where(jnp.where(off >= 0, off, NA_ROWS) < NA_ROWS, 0.0, NEG_INF)
            shift_idx[ta, tb] = (rs - row0 + 2 * (tb - ta) + NA_SHIFT0) // 2
    n_ctx = kc_ref.shape[1]

    def scores(h):
        k_all = jnp.concatenate([r[h] for r in k_refs] + [kc_ref[h]], axis=0)
        bias = jnp.concatenate(
            [jnp.concatenate([tab_ref[h, shift_idx[ta, tb]] + mask_ref[ta, tb] for tb in range(NA_KEY_BLOCKS)]
                             + [jnp.zeros((NA_BLOCK, n_ctx), F32)], axis=1)
             for ta in range(NA_Q_TILES)], axis=0)
        return _dot_nt(q_ref[h], k_all) + bias

    def finish(h, s):
        v_all = jnp.concatenate([r[h] for r in v_refs] + [vc_ref[h]], axis=0)
        oh_ref[h] = _softmax2_pv(s, v_all)

    _head_pipeline(N_Q_HEADS, scores, finish, s_ref)
    o_ref[...] = jnp.concatenate([oh_ref[h] for h in range(N_Q_HEADS)], axis=-1).astype(o_ref.dtype)


def _na_call(q, k, v, kc, vc, table):
    nh, b, s, _ = q.shape
    c = kc.shape[2]
    rows = s // GRID_W
    blk = lambda t: pl.BlockSpec(
        (nh, None, NA_BLOCK, HEAD_DIM), lambda bi, i, t=t: (0, bi, _na_first_key_row(i, rows) // 2 + t, 0))
    ctx = pl.BlockSpec((nh, None, c, HEAD_DIM), lambda bi, i: (0, bi, 0, 0))
    tq = NA_Q_TILES * NA_BLOCK
    return pl.pallas_call(
        functools.partial(_na_kernel, rows=rows), grid=(b, s // tq),
        in_specs=([pl.BlockSpec((nh, None, tq, HEAD_DIM), lambda bi, i: (0, bi, i, 0))]
                  + [blk(t) for t in range(NA_KEY_BLOCKS)] * 2
                  + [ctx, ctx, pl.BlockSpec(table.shape, lambda bi, i: (0, 0, 0, 0))]),
        out_specs=pl.BlockSpec((None, tq, nh * HEAD_DIM), lambda bi, i: (bi, i, 0)),
        out_shape=jax.ShapeDtypeStruct((b, s, nh * HEAD_DIM), BF16),
        scratch_shapes=[pltpu.VMEM((2, tq, NA_KEY_BLOCKS * NA_BLOCK + c), F32),
                        pltpu.VMEM((NA_Q_TILES, NA_KEY_BLOCKS, NA_BLOCK, NA_BLOCK), F32),
                        pltpu.VMEM((nh, tq, HEAD_DIM), F32)],
        compiler_params=_params(("parallel", "arbitrary")), name="nbr_attn",
    )(q, *([k] * NA_KEY_BLOCKS), *([v] * NA_KEY_BLOCKS), kc, vc, table)


def _hyena_features(idx, n):
    bands = (HYENA_EMB - 1) // 2
    idx_f = idx.astype(F32)
    t = idx_f / (n - 1)
    w = 2.0 * math.pi * idx_f / n
    f = jnp.linspace(1e-4, bands - 1, bands, dtype=F32)[None, :]
    fw = f * w[:, None]
    z = jnp.concatenate([t[:, None], jnp.cos(fw), -jnp.sin(fw)], axis=-1)
    return jnp.pad(z, ((0, 0), (0, HYENA_HIDDEN - HYENA_EMB)))


def _filter_kernel(z_ref, w1_ref, b1_ref, w2_ref, b2_ref, fr_ref, w3_ref, dl_ref, taps_ref, ss_ref, *, tm, zero_tile):
    i = pl.program_id(0)
    zf = z_ref[...]
    fr = fr_ref[...]
    h1 = jnp.sin(fr * (_dot(zf, w1_ref[...], HIGHEST) + b1_ref[...]))
    h2 = jnp.sin(fr * (_dot(h1, w2_ref[...], HIGHEST) + b2_ref[...]))
    filt = _dot(h2.astype(BF16), w3_ref[...])
    decay = jnp.exp(-zf[:, 0:1] * dl_ref[...])
    row = lax.broadcasted_iota(jnp.int32, (tm, 1), 0)
    kill = jnp.where(i == zero_tile, 0, -1)

    @pl.when(i == 0)
    def _():
        ss_ref[...] = jnp.zeros_like(ss_ref)

    for o in range(HYENA_ORDER):
        tp = jnp.where(row == kill, 0.0, filt[:, o * HYENA_WIDTH:(o + 1) * HYENA_WIDTH] * decay)
        taps_ref[o] = tp.reshape(taps_ref.shape[1:])
        ss_ref[o] += jnp.sum(tp * tp, axis=0, keepdims=True)


def _filter_call(n, w1, b1, w2, b2, freq, w3, *, tm, t_layout):
    n_tiles = 2 * n // tm
    pos = jnp.arange(2 * n, dtype=jnp.int32)
    idx = jnp.where(pos < n, pos, 2 * n - pos)
    z = _hyena_features(idx, n)
    w1p = jnp.pad(w1, ((0, HYENA_HIDDEN - HYENA_EMB), (0, 0)))
    w3d = w3.reshape(HYENA_HIDDEN, HYENA_ORDER, 2, HYENA_WIDTH).transpose(2, 0, 1, 3).reshape(
        2, HYENA_HIDDEN, HYENA_ORDER * HYENA_WIDTH).astype(BF16)
    deltas = np.abs(np.linspace(math.log(HYENA_DECAY_TARGET) / HYENA_LONG_DECAY_PCT,
                                math.log(HYENA_DECAY_TARGET) / HYENA_SHORT_DECAY_PCT, HYENA_WIDTH))
    half = n_tiles // 2
    small = lambda a: pl.BlockSpec(a.shape, lambda i: (0,) * a.ndim)
    b1r, b2r, frr = b1.reshape(1, -1), b2.reshape(1, -1), freq.reshape(1, -1)
    dl = jnp.asarray(deltas.reshape(1, -1), F32)
    if t_layout:
        taps_shape = (HYENA_ORDER, tm // SUB, n_tiles * SUB, HYENA_WIDTH)
        taps_spec = pl.BlockSpec((HYENA_ORDER, tm // SUB, SUB, HYENA_WIDTH), lambda i: (0, 0, i, 0))
    else:
        taps_shape = (HYENA_ORDER, 2 * n, HYENA_WIDTH)
        taps_spec = pl.BlockSpec((HYENA_ORDER, tm, HYENA_WIDTH), lambda i: (0, i, 0))
    return pl.pallas_call(
        functools.partial(_filter_kernel, tm=tm, zero_tile=half), grid=(n_tiles,),
        in_specs=[pl.BlockSpec((tm, HYENA_HIDDEN), lambda i: (i, 0)), small(w1p), small(b1r), small(w2), small(b2r),
                  small(frr), pl.BlockSpec((None, HYENA_HIDDEN, HYENA_ORDER * HYENA_WIDTH), lambda i: (i // half, 0, 0)),
                  small(dl)],
        out_specs=[taps_spec, pl.BlockSpec((HYENA_ORDER, 1, HYENA_WIDTH), lambda i: (0, 0, 0))],
        out_shape=[jax.ShapeDtypeStruct(taps_shape, F32), jax.ShapeDtypeStruct((HYENA_ORDER, 1, HYENA_WIDTH), F32)],
        compiler_params=_params(("arbitrary",)), name="hyena_filter",
    )(z, w1p, b1r, w2, b2r, frr, w3d, dl)


def _dft_mats(n1, n2):
    n = n1 * n2
    a1 = 2 * np.pi * np.outer(np.arange(n1), np.arange(n1)) / n1
    c1, s1 = np.cos(a1), np.sin(a1)
    a2 = 2 * np.pi * np.outer(np.arange(n2), np.arange(n2)) / n2
    c2, s2 = np.cos(a2), np.sin(a2)
    h = n1 // 2
    mats = dict(
        sig=np.block([[c1[:, :h], s1[:, :h]], [-s1[:, :h], c1[:, :h]]]),
        filt=np.concatenate([c1, -s1], axis=0),
        fwd2=np.block([[c2, s2], [-s2, c2]]),
        inv2=np.block([[c2, -s2], [s2, c2]]),
        last=np.block([[c1[:h, :], -s1[:h, :]], [s1[:h, :], c1[:h, :]]]) / n,
    )
    mats = {k: jnp.asarray(v, F32).astype(BF16) for k, v in mats.items()}
    at = (jnp.arange(n1, dtype=jnp.int32)[:, None] * jnp.arange(n2, dtype=jnp.int32)[None, :]).astype(F32) * (2 * math.pi / n)
    tw = tuple(jnp.broadcast_to(f(at)[:, :, None], (n1, n2, LANES)) for f in (jnp.cos, jnp.sin))
    return mats, tw


def _stage1_kernel(m_ref, x_ref, o_ref, *, rows_in, rows_out, complex_in):
    for r in range(SUB):
        sl = pl.ds(r, rows_in, stride=SUB)
        if complex_in:
            x = jnp.concatenate([x_ref[0, sl, :], x_ref[1, sl, :]], axis=0)
        else:
            x = x_ref[sl, :]
        o_ref[pl.ds(r, rows_out, stride=SUB), :] = _dot(m_ref[...], x.astype(BF16))


def _stage1_call(mat, x, sel, *, complex_in, name):
    w = HYENA_WIDTH
    nl = w // LANES
    _, groups, rows8, _ = x.shape
    if complex_in:
        spec = pl.BlockSpec((2, None, rows8, LANES), lambda j, c: (0, j, 0, sel * nl + c))
    else:
        spec = pl.BlockSpec((None, None, rows8, LANES), lambda j, c: (sel, j, 0, c))
    rows_out = mat.shape[0]
    kern = functools.partial(_stage1_kernel, rows_in=rows8 // SUB, rows_out=rows_out, complex_in=complex_in)
    return pl.pallas_call(
        kern, grid=(groups, nl),
        in_specs=[pl.BlockSpec(mat.shape, lambda j, c: (0, 0)), spec],
        out_specs=pl.BlockSpec((None, rows_out * SUB, LANES), lambda j, c: (j, 0, c)),
        out_shape=jax.ShapeDtypeStruct((groups, rows_out * SUB, w), F32),
        compiler_params=_params(("parallel", "parallel")), name=name,
    )(mat, x)


def _lane_tile(t):
    return jnp.concatenate([t] * (HYENA_WIDTH // LANES), axis=-1)


def _twiddle_dft2(ar, ai, tc, ts, f_ref, n2):
    br = ar * tc + ai * ts
    bi = ai * tc - ar * ts
    x = _dot(f_ref[...], jnp.concatenate([br, bi], axis=0).astype(BF16))
    return x[:n2], x[n2:]


def _mid_kernel(f_ref, g_ref, ar_ref, ai_ref, fr_ref, fi_ref, tc_ref, ts_ref, ss_ref, or_ref, oi_ref, *, n2):
    tc, ts = _lane_tile(tc_ref[...]), _lane_tile(ts_ref[...])
    w = ar_ref.shape[-1]
    tc2, ts2 = jnp.concatenate([tc, tc], axis=1), jnp.concatenate([ts, ts], axis=1)
    ar = jnp.concatenate([ar_ref[...].reshape(n2, w), fr_ref[...].reshape(n2, w)], axis=1)
    ai = jnp.concatenate([ai_ref[...].reshape(n2, w), fi_ref[...].reshape(n2, w)], axis=1)
    xr, xi = _twiddle_dft2(ar, ai, tc2, ts2, f_ref, n2)
    norm = lax.rsqrt(ss_ref[...] + EPS)
    hr, hi = xr[:, w:] * norm, xi[:, w:] * norm
    xr, xi = xr[:, :w], xi[:, :w]
    yr = xr * hr - xi * hi
    yi = xr * hi + xi * hr
    y = _dot(g_ref[...], jnp.concatenate([yr, yi], axis=0).astype(BF16))
    cr, ci = y[:n2], y[n2:]
    or_ref[...] = (cr * tc - ci * ts).reshape(or_ref.shape)
    oi_ref[...] = (ci * tc + cr * ts).reshape(oi_ref.shape)


def _mid_call(fwd2, inv2, a, af, twc, tws, ss, order):
    groups, rows8, w = a.shape
    n2, n1 = groups * SUB, rows8 // (2 * SUB)
    col = lambda off: pl.BlockSpec((groups, SUB, w), lambda k: (0, k + off, 0))
    tw = pl.BlockSpec((None, n2, LANES), lambda k: (k, 0, 0))
    mat = pl.BlockSpec(fwd2.shape, lambda k: (0, 0))
    return pl.pallas_call(
        functools.partial(_mid_kernel, n2=n2), grid=(n1,),
        in_specs=[mat, mat, col(0), col(n1), col(0), col(n1), tw, tw,
                  pl.BlockSpec((None, 1, w), lambda k: (order, 0, 0))],
        out_specs=[col(0)] * 2,
        out_shape=[jax.ShapeDtypeStruct((groups, n1 * SUB, w), F32)] * 2,
        compiler_params=_params(("parallel",)), name="hyena_conv_mid",
    )(fwd2, inv2, a, a, af, af, twc, tws, ss)


def _last_kernel(m_ref, br_ref, bi_ref, z_ref, x_ref, skip_ref, o_ref, *, n1):
    skip = skip_ref[...]
    half = n1 // 2
    for r in range(SUB):
        full, part = pl.ds(r, n1, stride=SUB), pl.ds(r, half, stride=SUB)
        y = _dot(m_ref[...], jnp.concatenate([br_ref[full, :], bi_ref[full, :]], axis=0).astype(BF16))
        o_ref[0, part, :] = x_ref[0, part, :] * (y[:half] + z_ref[0, part, :] * skip)
        o_ref[1, part, :] = x_ref[1, part, :] * (y[half:] + z_ref[1, part, :] * skip)


def _last_call(mat, br, bi, zsrc, zcol, xsrc, xcol, skip, order):
    groups, rows8, w = br.shape
    n1 = rows8 // SUB
    nl = w // LANES
    b_spec = pl.BlockSpec((None, rows8, LANES), lambda j, c: (j, 0, c))
    seq = lambda col: pl.BlockSpec((2, None, rows8 // 2, LANES), lambda j, c: (0, j, 0, col * nl + c))
    return pl.pallas_call(
        functools.partial(_last_kernel, n1=n1), grid=(groups, nl),
        in_specs=[pl.BlockSpec(mat.shape, lambda j, c: (0, 0)), b_spec, b_spec, seq(zcol), seq(xcol),
                  pl.BlockSpec((None, 1, LANES), lambda j, c: (order, 0, c))],
        out_specs=seq(0),
        out_shape=jax.ShapeDtypeStruct((2, groups, rows8 // 2, w), F32),
        compiler_params=_params(("parallel", "parallel")), name="hyena_conv_last",
    )(mat, br, bi, zsrc, xsrc, skip)


def _hyena_long(uct, taps_t, ss, skip):
    n2 = uct.shape[1] * SUB
    mats, (twc, tws) = _dft_mats(FFT_N1, n2)
    skip3 = skip.reshape(HYENA_ORDER, 1, HYENA_WIDTH)
    z, zcol = uct, 0
    for o in range(HYENA_ORDER):
        af = _stage1_call(mats["filt"], taps_t, o, complex_in=False, name="hyena_filter_dft1")
        a = _stage1_call(mats["sig"], z, zcol, complex_in=True, name="hyena_conv_first")
        br, bi = _mid_call(mats["fwd2"], mats["inv2"], a, af, twc, tws, ss, o)
        z = _last_call(mats["last"], br, bi, z, zcol, uct, o + 1, skip3, o)
        zcol = 0
    return z


def _ctx_hyena_kernel(v_ref, x1_ref, x2_ref, taps_ref, ss_ref, skip_ref, fh_ref, fz_ref, fi_ref, o_ref, *, n):
    nn = 2 * n
    stack = lambda r: jnp.concatenate([r[0], r[1]], axis=0)
    z = stack(v_ref)
    for o, x_ref in enumerate((x1_ref, x2_ref)):
        h = _dot(fh_ref[...], taps_ref[o], HIGHEST) * lax.rsqrt(ss_ref[o] + EPS)
        hr, hi = h[:nn], h[nn:]
        zf = _dot(fz_ref[...], z, HIGHEST)
        zr, zi = zf[:nn], zf[nn:]
        y = _dot(fi_ref[...], jnp.concatenate([zr * hr - zi * hi, zr * hi + zi * hr], axis=0), HIGHEST)
        z = stack(x_ref) * (y + z * skip_ref[o])
    o_ref[0] = z[:n].astype(o_ref.dtype)
    o_ref[1] = z[n:].astype(o_ref.dtype)


def _ctx_hyena_call(u, taps, ss, skip):
    _, n, _ = u.shape
    nn = 2 * n
    ang = 2 * np.pi * np.outer(np.arange(nn), np.arange(nn)) / nn
    c, s = np.cos(ang), np.sin(ang)
    fh = jnp.asarray(np.concatenate([c, -s], axis=0), F32)
    fz = jnp.asarray(np.block([[c[:, :n], s[:, :n]], [-s[:, :n], c[:, :n]]]), F32)
    fi = jnp.asarray(np.block([[c[:n, :], -s[:n, :]], [s[:n, :], c[:n, :]]]) / nn, F32)
    nchunk = HYENA_WIDTH // LANES
    skip3 = skip.reshape(HYENA_ORDER, 1, HYENA_WIDTH)
    small = lambda a: pl.BlockSpec(a.shape, lambda j: (0,) * a.ndim)
    part = lambda g: pl.BlockSpec((2, n, LANES), lambda j, g=g: (0, 0, g * nchunk + j))

    return pl.pallas_call(
        functools.partial(_ctx_hyena_kernel, n=n), grid=(nchunk,),
        in_specs=[part(0), part(1), part(2),
                  pl.BlockSpec((HYENA_ORDER, nn, LANES), lambda j: (0, 0, j)),
                  pl.BlockSpec((HYENA_ORDER, 1, LANES), lambda j: (0, 0, j)),
                  pl.BlockSpec((HYENA_ORDER, 1, LANES), lambda j: (0, 0, j)),
                  small(fh), small(fz), small(fi)],
        out_specs=pl.BlockSpec((2, n, LANES), lambda j: (0, 0, j)),
        out_shape=jax.ShapeDtypeStruct((2, n, HYENA_WIDTH), BF16),
        compiler_params=_params(("parallel",)), name="hyena_ctx",
    )(u, u, u, taps, ss, skip3, fh, fz, fi)


def _outproj_kernel(x_ref, ya_ref, yb_ref, g_ref, w_ref, gate_ref, o_ref):
    tm, half = ya_ref.shape
    ya = (ya_ref[...].astype(F32) * g_ref[:, :half].astype(F32)).astype(BF16)
    yb = (yb_ref[...].reshape(tm, half).astype(F32) * g_ref[:, half:].astype(F32)).astype(BF16)
    upd = _dot(ya, w_ref[:half, :]) + _dot(yb, w_ref[half:, :])
    o_ref[...] = x_ref[...] + gate_ref[...] * upd


def _outproj_call(x2d, ya, yb, gates, w_bf, gate_vec, *, seq_len, tm, per_batch_mod, yb_t_layout=False):
    r, d = x2d.shape
    tps = seq_len // tm
    half = ya.shape[-1]
    if yb_t_layout:
        yb_spec = pl.BlockSpec((None, tm // SUB, SUB, half), lambda i: (i // tps, 0, i % tps, 0))
    else:
        yb_spec = pl.BlockSpec((tm, half), lambda i: (i, 0))
    mod_map = (lambda i: (i // tps, 0, 0)) if per_batch_mod else (lambda i: (0, 0, 0))
    return pl.pallas_call(
        _outproj_kernel, grid=(r // tm,),
        in_specs=[pl.BlockSpec((tm, d), lambda i: (i, 0)), pl.BlockSpec((tm, half), lambda i: (i, 0)), yb_spec,
                  pl.BlockSpec((tm, 2 * half), lambda i: (i, 0)), pl.BlockSpec(w_bf.shape, lambda i: (0, 0)),
                  pl.BlockSpec((None, 1, d), mod_map)],
        out_specs=pl.BlockSpec((tm, d), lambda i: (i, 0)),
        out_shape=jax.ShapeDtypeStruct((r, d), F32),
        compiler_params=_params(("parallel",)), name="outproj",
    )(x2d, ya, yb, gates, w_bf, gate_vec)


def _rope_tables(n_tokens):
    t = jnp.arange(n_tokens, dtype=jnp.int32)
    row = (t // GRID_W).astype(F32)
    col = (t % GRID_W).astype(F32)
    n_pairs = HEAD_DIM // 4
    inv = ROPE_THETA ** (-jnp.arange(n_pairs, dtype=F32) / n_pairs)
    ang = jnp.concatenate([row[:, None] * inv, col[:, None] * inv], axis=-1)
    cos = jnp.repeat(jnp.cos(ang), 2, axis=-1)
    sin = jnp.repeat(jnp.sin(ang), 2, axis=-1)
    even = (jnp.arange(HEAD_DIM) % 2 == 0)[None, :]
    tile = lambda a: jnp.tile(a, (1, LANES // HEAD_DIM))
    return tile(cos), tile(jnp.where(even, -sin, 0.0)), tile(jnp.where(even, 0.0, sin))


def _gain_rows(*gains):
    rows = [jnp.tile(g, LANES // HEAD_DIM) for g in gains]
    rows += [jnp.zeros((LANES,), F32)] * (8 - len(rows))
    return jnp.stack(rows)


def kernel(x, c, ctx, c_ctx, norm_g, w_ada, b_ada, w_in, w_out, glob_q_norm, glob_k_norm, hy_conv_w, hy_conv_b, hy_w1, hy_b1, hy_w2, hy_b2, hy_freq, hy_w3, hy_skip, win_q_norm, win_k_norm, win_sink, nat_q_norm, nat_k_norm, nat_rpb):
    b, s, d = x.shape
    n_ctx = ctx.shape[1]
    n2 = 2 * s // FFT_N1
    tm = n2
    w_in_bf, w_out_bf = w_in.astype(BF16), w_out.astype(BF16)

    cond = jnp.zeros((8, d), F32).at[:b].set(c).at[b].set(c_ctx)
    mod = _ada_call(cond, w_ada, b_ada)
    shift, scale, gate = (mod[:, :, i * d:(i + 1) * d] for i in range(3))
    lat = lambda m, l: m[l, :b].reshape(b, 1, d)
    cx = lambda m, l: m[l, b:b + 1].reshape(1, 1, d)

    rope = _rope_tables(s)
    x2 = x.reshape(b * s, d)
    xc2 = ctx.reshape(b * n_ctx, d)
    heads4 = lambda a, n: a.reshape(a.shape[0], b, n, a.shape[-1])

    gains0 = _gain_rows(glob_q_norm[0], glob_k_norm[0])
    gw, kw = GROUP_WIDTH, N_KV_HEADS * HEAD_DIM
    hyw = (HYENA_ORDER + 1) * HYENA_WIDTH
    plan_lat = (("qk", 0, gw, 0, True, QK_SCALE2), ("qk", gw, kw, 1, True, 1.0), ("vones", gw + kw, kw),
                ("hy", gw + 2 * kw, hyw), ("gate", gw + 2 * kw + hyw, 2 * gw))
    plan_ctx = (("qk", 0, gw, 0, False, QK_SCALE2), ("qk", gw, kw, 1, False, 1.0), ("vones", gw + kw, kw),
                ("hy", gw + 2 * kw, hyw), ("gate", gw + 2 * kw + hyw, 2 * gw))
    conv = (hy_conv_w[0], hy_conv_b[0])
    q0, k0, v0, uct, g0 = _inproj_call(
        x2, lat(shift, 0), lat(scale, 0), norm_g[0], w_in_bf[0], plan_lat, seq_len=s, tm=tm, per_batch_mod=True,
        rope_tabs=rope, gains=gains0, conv=conv, hy_t_layout=True, name="inproj_even")
    qc, kc, vc, ucc, gc = _inproj_call(
        xc2, cx(shift, 0), cx(scale, 0), norm_g[0], w_in_bf[0], plan_ctx, seq_len=n_ctx, tm=n_ctx,
        per_batch_mod=False, gains=gains0, conv=conv, name="inproj_even_ctx")
    kc4, vc4 = heads4(kc, n_ctx), heads4(vc, n_ctx)
    ya = _flash_call(heads4(q0, s), kc4, vc4, heads4(k0, s), heads4(v0, s), tq=256, name="global_gqa")
    yca = _flash_call(heads4(qc, n_ctx), kc4, vc4, tq=n_ctx, name="ctx_gqa")

    filt = (hy_w1[0], hy_b1[0], hy_w2[0], hy_b2[0], hy_freq[0], hy_w3[0])
    taps_t, ss = _filter_call(s, *filt, tm=n2, t_layout=True)
    ybt = _hyena_long(uct, taps_t, ss, hy_skip[0])
    taps_c, ss_c = _filter_call(n_ctx, *filt, tm=n_ctx, t_layout=False)
    ycb = _ctx_hyena_call(ucc.reshape(b, n_ctx, hyw), taps_c, ss_c, hy_skip[0])

    x2 = _outproj_call(x2, ya.reshape(b * s, gw), ybt, g0, w_out_bf[0], lat(gate, 0), seq_len=s, tm=tm,
                       per_batch_mod=True, yb_t_layout=True)
    xc2 = _outproj_call(xc2, yca.reshape(b * n_ctx, gw), ycb.reshape(b * n_ctx, gw), gc, w_out_bf[0], cx(gate, 0),
                        seq_len=n_ctx, tm=n_ctx, per_batch_mod=False)

    gains1 = _gain_rows(win_q_norm[0], win_k_norm[0], nat_q_norm[0], nat_k_norm[0])
    o_nq = gw + 2 * kw
    plan_lat = (("qk", 0, gw, 0, True, QK_SCALE2), ("qk", gw, kw, 1, True, 1.0), ("v", gw + kw, kw),
                ("qk", o_nq, gw, 2, False, QK_SCALE2), ("qk", o_nq + gw, gw, 3, False, 1.0), ("v", o_nq + 2 * gw, gw),
                ("gate", o_nq + 3 * gw, 2 * gw))
    plan_ctx = (("qk", gw, kw, 1, False, 1.0), ("v", gw + kw, kw),
                ("qk", o_nq + gw, gw, 3, False, 1.0), ("v", o_nq + 2 * gw, gw))
    qw, kwl, vwl, qn, kn, vn, g1 = _inproj_call(
        x2, lat(shift, 1), lat(scale, 1), norm_g[1], w_in_bf[1], plan_lat, seq_len=s, tm=tm, per_batch_mod=True,
        rope_tabs=rope, gains=gains1, name="inproj_odd")
    kwc, vwc, knc, vnc = _inproj_call(
        xc2, cx(shift, 1), cx(scale, 1), norm_g[1], w_in_bf[1], plan_ctx, seq_len=n_ctx, tm=n_ctx,
        per_batch_mod=False, gains=gains1, name="inproj_odd_ctx")
    yw = _window_call(win_sink[0], heads4(qw, s), heads4(kwc, n_ctx), heads4(vwc, n_ctx), heads4(kwl, s), heads4(vwl, s))
    table = _rpb_table_call(nat_rpb[0])
    yn = _na_call(heads4(qn, s), heads4(kn, s), heads4(vn, s), heads4(knc, n_ctx), heads4(vnc, n_ctx), table)
    x2 = _outproj_call(x2, yw.reshape(b * s, gw), yn.reshape(b * s, gw), g1, w_out_bf[1], lat(gate, 1), seq_len=s,
                       tm=tm, per_batch_mod=True)
    return x2.reshape(b, s, d)
```

```python
import functools
import math

import numpy as np
import jax
import jax.numpy as jnp
from jax import lax
from jax.experimental import pallas as pl
from jax.experimental.pallas import tpu as pltpu

F32 = jnp.float32
BF16 = jnp.bfloat16
HIGHEST = lax.Precision.HIGHEST

GRID_W = 64
HEAD_DIM = 64
GROUP_WIDTH = 512
N_Q_HEADS = 8
N_KV_HEADS = 2
GQA_GROUP = N_Q_HEADS // N_KV_HEADS
HYENA_WIDTH = 512
HYENA_ORDER = 2
HYENA_EMB = 33
HYENA_HIDDEN = 64
HYENA_DECAY_TARGET = 1e-2
HYENA_SHORT_DECAY_PCT = 0.3
HYENA_LONG_DECAY_PCT = 1.5
WINDOW = 128
NA_ROWS = 8
NA_COLS = 16
ROPE_THETA = 10000.0
EPS = 1e-6
NEG_INF = -1e30
LOG2E = math.log2(math.e)
QK_SCALE2 = HEAD_DIM ** -0.5 * LOG2E

LANES = 128
SUB = 8
HALO = 16
FFT_N1 = 128
MXU_COLS = 256
FLASH_MAX_CHUNK = 1280
FLASH_UNROLL = 4
VMEM_LIMIT = 56 * 1024 * 1024


def _dot(a, b, precision=None):
    return jnp.dot(a, b, preferred_element_type=F32, precision=precision)


def _dot_nt(a, b):
    return lax.dot_general(a, b, (((1,), (1,)), ((), ())), preferred_element_type=F32)


def _params(sem):
    return pltpu.CompilerParams(dimension_semantics=sem, vmem_limit_bytes=VMEM_LIMIT)


def _silu(v):
    return v / (1.0 + jnp.exp(-v))


def _softmax2_pv(s, v, sink=None):
    m = jnp.max(s, axis=1, keepdims=True)
    if sink is not None:
        m = jnp.maximum(m, sink)
    p = jnp.exp2(s - m)
    l = jnp.sum(p, axis=1, keepdims=True)
    if sink is not None:
        l = l + jnp.exp2(sink - m)
    return _dot(p.astype(BF16), v) / l


def _ada_kernel(cond_ref, w_ref, b_ref, o_ref):
    o_ref[0] = _dot(_silu(cond_ref[...]), w_ref[0], HIGHEST) + b_ref[0]


def _ada_call(cond, w_ada, b_ada):
    depth, d, d3 = w_ada.shape
    tn = 1024
    return pl.pallas_call(
        _ada_kernel,
        grid=(depth, d3 // tn),
        in_specs=[pl.BlockSpec((8, d), lambda l, j: (0, 0)),
                  pl.BlockSpec((1, d, tn), lambda l, j: (l, 0, j)),
                  pl.BlockSpec((1, 1, tn), lambda l, j: (l, 0, j))],
        out_specs=pl.BlockSpec((1, 8, tn), lambda l, j: (l, 0, j)),
        out_shape=jax.ShapeDtypeStruct((depth, 8, d3), F32),
        compiler_params=_params(("parallel", "parallel")),
        name="adaln",
    )(cond, w_ada, b_ada.reshape(depth, 1, d3))


def _inproj_kernel(*refs, plan, tm, tps, has_halo, has_rope):
    it = iter(refs)
    x_ref = next(it)
    if has_halo:
        xp_ref, xn_ref = next(it), next(it)
    sh_ref, sc_ref, g_ref, w_ref = next(it), next(it), next(it), next(it)
    if has_rope:
        cos_ref, s1_ref, s2_ref = next(it), next(it), next(it)
    gains_ref, mavg_ref = next(it), next(it)
    if has_halo:
        cw_ref, cb_ref = next(it), next(it)
    outs = [next(it) for _ in plan]
    if has_halo:
        xe_ref, hu_ref = next(it), next(it)

    g, sh, sc = g_ref[...], sh_ref[...], sc_ref[...]

    def modnorm(xv):
        ms = jnp.mean(xv * xv, axis=-1, keepdims=True)
        return (xv * lax.rsqrt(ms + EPS) * g) * (1.0 + sc) + sh

    h = modnorm(x_ref[...]).astype(BF16)
    mavg = mavg_ref[...]

    for seg, o_ref in zip(plan, outs):
        kind, start, width = seg[0], seg[1], seg[2]
        if kind == "hy":
            xe_ref[0:HALO] = modnorm(xp_ref[...]).astype(BF16)
            xe_ref[HALO:HALO + tm] = h
            xe_ref[HALO + tm:HALO + tm + HALO] = modnorm(xn_ref[...]).astype(BF16)
            hu_ref[...] = _dot(xe_ref[...], w_ref[:, start:start + width])
            pos = pl.program_id(0) % tps
            row = lax.broadcasted_iota(jnp.int32, (tm, 1), 0)
            kill_prev = jnp.where(pos == 0, 0, -1)
            kill_next = jnp.where(pos == tps - 1, tm - 1, -1)
            a = jnp.where(row == kill_prev, 0.0, hu_ref[pl.ds(HALO - 1, tm), :])
            b = hu_ref[pl.ds(HALO, tm), :]
            c = jnp.where(row == kill_next, 0.0, hu_ref[pl.ds(HALO + 1, tm), :])
            uc = a * cw_ref[0:1, :] + b * cw_ref[1:2, :] + c * cw_ref[2:3, :] + cb_ref[...]
            o_ref[...] = uc.reshape(o_ref.shape)
            continue
        pv = _dot(h, w_ref[:, start:start + width])
        if kind == "gate":
            o_ref[...] = _silu(pv).astype(o_ref.dtype)
        elif kind == "v":
            for hd in range(width // HEAD_DIM):
                o_ref[hd] = pv[:, hd * HEAD_DIM:(hd + 1) * HEAD_DIM].astype(o_ref.dtype)
        elif kind == "vones":
            lane = lax.broadcasted_iota(jnp.int32, (tm, LANES - HEAD_DIM), 1)
            tail = jnp.where(lane == 0, 1.0, 0.0)
            for hd in range(width // HEAD_DIM):
                o_ref[hd] = jnp.concatenate([pv[:, hd * HEAD_DIM:(hd + 1) * HEAD_DIM], tail], axis=1).astype(o_ref.dtype)
        else:
            gain_row, rope, scale = seg[3], seg[4], seg[5]
            gain = gains_ref[gain_row:gain_row + 1, :]
            for c in range(width // LANES):
                v = pv[:, c * LANES:(c + 1) * LANES]
                sq = v * v
                hi = sq.astype(BF16)
                lo = (sq - hi.astype(F32)).astype(BF16)
                ms = _dot(hi, mavg) + _dot(lo, mavg)
                vn = v * lax.rsqrt(ms + EPS) * gain
                if rope:
                    vn = (vn * cos_ref[...] + pltpu.roll(vn, LANES - 1, 1) * s1_ref[...]
                          + pltpu.roll(vn, 1, 1) * s2_ref[...])
                vn = (vn * scale).astype(o_ref.dtype)
                o_ref[2 * c] = vn[:, :HEAD_DIM]
                o_ref[2 * c + 1] = vn[:, HEAD_DIM:]


def _inproj_call(x2d, shift, scale, norm_g, w_bf, plan, *, seq_len, tm, per_batch_mod,
                 rope_tabs=None, gains=None, conv=None, hy_t_layout=False, name="inproj"):
    r, d = x2d.shape
    tps = seq_len // tm
    n_tiles = r // tm
    n_seq = r // seq_len
    has_halo = any(s[0] == "hy" for s in plan)
    has_rope = rope_tabs is not None
    hb = tm // HALO
    mod_map = (lambda i: (i // tps, 0, 0)) if per_batch_mod else (lambda i: (0, 0, 0))

    args, specs = [x2d], [pl.BlockSpec((tm, d), lambda i: (i, 0))]
    if has_halo:
        args += [x2d, x2d]
        specs += [pl.BlockSpec((HALO, d), lambda i: (jnp.maximum(i * hb - 1, 0), 0)),
                  pl.BlockSpec((HALO, d), lambda i: (jnp.minimum((i + 1) * hb, r // HALO - 1), 0))]
    args += [shift, scale, norm_g.reshape(1, d), w_bf]
    specs += [pl.BlockSpec((None, 1, d), mod_map), pl.BlockSpec((None, 1, d), mod_map),
              pl.BlockSpec((1, d), lambda i: (0, 0)),
              pl.BlockSpec(w_bf.shape, lambda i: (0, 0))]
    if has_rope:
        args += list(rope_tabs)
        specs += [pl.BlockSpec((tm, LANES), lambda i: (i % tps, 0))] * 3
    mavg = np.kron(np.eye(LANES // HEAD_DIM), np.full((HEAD_DIM, HEAD_DIM), 1.0 / HEAD_DIM))
    args += [gains, jnp.asarray(mavg, BF16)]
    specs += [pl.BlockSpec(gains.shape, lambda i: (0, 0)), pl.BlockSpec((LANES, LANES), lambda i: (0, 0))]
    if has_halo:
        cw, cb = conv
        args += [cw, cb.reshape(1, -1)]
        specs += [pl.BlockSpec(cw.shape, lambda i: (0, 0)), pl.BlockSpec((1, cb.shape[-1]), lambda i: (0, 0))]

    out_shapes, out_specs = [], []
    for seg in plan:
        kind, width = seg[0], seg[2]
        if kind == "hy":
            if hy_t_layout:
                out_shapes.append(jax.ShapeDtypeStruct((n_seq, tm // SUB, tps * SUB, width), F32))
                out_specs.append(pl.BlockSpec((None, tm // SUB, SUB, width), lambda i: (i // tps, 0, i % tps, 0)))
            else:
                out_shapes.append(jax.ShapeDtypeStruct((r, width), F32))
                out_specs.append(pl.BlockSpec((tm, width), lambda i: (i, 0)))
        elif kind == "gate":
            out_shapes.append(jax.ShapeDtypeStruct((r, width), BF16))
            out_specs.append(pl.BlockSpec((tm, width), lambda i: (i, 0)))
        else:
            nh = width // HEAD_DIM
            hw = LANES if kind == "vones" else HEAD_DIM
            out_shapes.append(jax.ShapeDtypeStruct((nh, r, hw), BF16))
            out_specs.append(pl.BlockSpec((nh, tm, hw), lambda i: (0, i, 0)))
    scratch = []
    if has_halo:
        hw = [s[2] for s in plan if s[0] == "hy"][0]
        scratch = [pltpu.VMEM((tm + 2 * HALO, d), BF16), pltpu.VMEM((tm + 2 * HALO, hw), F32)]
    kern = functools.partial(_inproj_kernel, plan=plan, tm=tm, tps=tps, has_halo=has_halo, has_rope=has_rope)
    return pl.pallas_call(
        kern, grid=(n_tiles,), in_specs=specs, out_specs=out_specs, out_shape=out_shapes,
        scratch_shapes=scratch, compiler_params=_params(("parallel",)), name=name,
    )(*args)


def _flash_kernel(q_ref, k_ref, v_ref, o_ref, m_ref, acc_ref, s_ref, *, tq, tk, n_chunks, unroll):
    rows = GQA_GROUP * tq
    q = q_ref[...].reshape(rows, HEAD_DIM)
    m_ref[...] = jnp.full((rows, LANES), NEG_INF, F32)
    acc_ref[...] = jnp.zeros((rows, LANES), F32)

    def scores(j):
        return _dot_nt(q, k_ref[pl.ds(pl.multiple_of(j * tk, tk), tk), :])

    def softmax_pv(s, j):
        m_prev = m_ref[...]
        m_new = jnp.maximum(m_prev, jnp.max(s, axis=1, keepdims=True))
        m_ref[...] = m_new
        p = jnp.exp2(s - jnp.concatenate([m_new] * (tk // LANES), axis=1)).astype(BF16)
        pv = _dot(p, v_ref[pl.ds(pl.multiple_of(j * tk, tk), tk), :])
        acc_ref[...] = jnp.exp2(m_prev - m_new) * acc_ref[...] + pv

    s_ref[0] = scores(0)
    if n_chunks > 1:
        def half(j, cur):
            s_ref[1 - cur] = scores(j + 1)
            softmax_pv(s_ref[cur], j)

        def body(i, carry):
            for u in range(unroll):
                half(unroll * i + u, u % 2)
            return carry
        lax.fori_loop(0, (n_chunks - 1) // unroll, body, 0)
    softmax_pv(s_ref[(n_chunks - 1) % 2], n_chunks - 1)
    acc = acc_ref[...]
    o = acc[:, :HEAD_DIM] / acc[:, HEAD_DIM:HEAD_DIM + 1]
    o_ref[...] = jnp.concatenate([o[gi * tq:(gi + 1) * tq] for gi in range(GQA_GROUP)], axis=-1).astype(o_ref.dtype)


def _flash_chunk(n_keys):
    for tk in range(FLASH_MAX_CHUNK, 0, -MXU_COLS):
        if n_keys % tk == 0 and (n_keys // tk == 1 or (n_keys // tk - 1) % 2 == 0):
            return tk
    raise ValueError(f"no key chunk for {n_keys} keys")


def _flash_call(q, k, v, *, tq, name="gqa"):
    _, b, sq, _ = q.shape
    n = k.shape[2]
    tk = _flash_chunk(n)
    n_chunks = n // tk
    unroll = math.gcd(FLASH_UNROLL, n_chunks - 1) if n_chunks > 1 else 0
    kv_spec = lambda w: pl.BlockSpec((None, None, n, w), lambda bi, j, i: (j, bi, 0, 0))
    rows = GQA_GROUP * tq
    kern = functools.partial(_flash_kernel, tq=tq, tk=tk, n_chunks=n_chunks, unroll=unroll)
    return pl.pallas_call(
        kern, grid=(b, N_KV_HEADS, sq // tq),
        in_specs=[pl.BlockSpec((GQA_GROUP, None, tq, HEAD_DIM), lambda bi, j, i: (j, bi, i, 0)),
                  kv_spec(HEAD_DIM), kv_spec(LANES)],
        out_specs=pl.BlockSpec((None, tq, GQA_GROUP * HEAD_DIM), lambda bi, j, i: (bi, i, j)),
        out_shape=jax.ShapeDtypeStruct((b, sq, GROUP_WIDTH), BF16),
        scratch_shapes=[pltpu.VMEM((rows, LANES), F32), pltpu.VMEM((rows, LANES), F32),
                        pltpu.VMEM((2 if n_chunks > 1 else 1, rows, tk), F32)],
        compiler_params=_params(("parallel", "parallel", "arbitrary")), name=name,
    )(q, k, v)


def _head_pipeline(n_heads, scores_fn, finish_fn, s_ref):
    s_ref[0] = scores_fn(0)
    for h in range(n_heads - 1):
        s_ref[(h + 1) % 2] = scores_fn(h + 1)
        finish_fn(h, s_ref[h % 2])
    finish_fn(n_heads - 1, s_ref[(n_heads - 1) % 2])


def _window_kernel(sink_ref, q_ref, kc_ref, vc_ref, kl_ref, vl_ref, o_ref, s_ref, mask_ref, oh_ref, *, tq, span, seq):
    j, i = pl.program_id(1), pl.program_id(2)
    start = pl.multiple_of(jnp.clip(i * tq - WINDOW, 0, seq - span), LANES)
    n_ctx = kc_ref.shape[0]
    qpos = i * tq + lax.broadcasted_iota(jnp.int32, (tq, span), 0)
    kpos = start + lax.broadcasted_iota(jnp.int32, (tq, span), 1)
    mask_ref[...] = jnp.concatenate([jnp.where(jnp.abs(qpos - kpos) <= WINDOW, 0.0, NEG_INF),
                                     jnp.zeros((tq, n_ctx), F32)], axis=1)

    def scores(g):
        k_all = jnp.concatenate([kl_ref[pl.ds(start, span), :], kc_ref[...]], axis=0)
        return _dot_nt(q_ref[g], k_all) + mask_ref[...]

    def finish(g, s):
        v_all = jnp.concatenate([vl_ref[pl.ds(start, span), :], vc_ref[...]], axis=0)
        oh_ref[g] = _softmax2_pv(s, v_all, sink_ref[j * GQA_GROUP + g] * LOG2E)

    _head_pipeline(GQA_GROUP, scores, finish, s_ref)
    o_ref[...] = jnp.concatenate([oh_ref[g] for g in range(GQA_GROUP)], axis=-1).astype(o_ref.dtype)


def _window_call(sink, q, kc, vc, kl, vl, *, tq=256):
    _, b, s, _ = q.shape
    c = kc.shape[2]
    span = tq + 2 * WINDOW
    kern = functools.partial(_window_kernel, tq=tq, span=span, seq=s)
    full = lambda n: pl.BlockSpec((None, None, n, HEAD_DIM), lambda bi, j, i: (j, bi, 0, 0))
    return pl.pallas_call(
        kern, grid=(b, N_KV_HEADS, s // tq),
        in_specs=[pl.BlockSpec(memory_space=pltpu.SMEM),
                  pl.BlockSpec((GQA_GROUP, None, tq, HEAD_DIM), lambda bi, j, i: (j, bi, i, 0)),
                  full(c), full(c), full(s), full(s)],
        out_specs=pl.BlockSpec((None, tq, GQA_GROUP * HEAD_DIM), lambda bi, j, i: (bi, i, j)),
        out_shape=jax.ShapeDtypeStruct((b, s, GROUP_WIDTH), BF16),
        scratch_shapes=[pltpu.VMEM((2, tq, span + c), F32), pltpu.VMEM((tq, span + c), F32),
                        pltpu.VMEM((GQA_GROUP, tq, HEAD_DIM), F32)],
        compiler_params=_params(("parallel", "parallel", "arbitrary")), name="window_gqa",
    )(sink, q, kc, vc, kl, vl)


NA_BLOCK = 2 * GRID_W
NA_Q_TILES = 2
NA_KEY_BLOCKS = NA_Q_TILES + NA_ROWS // 2
NA_SHIFT0 = NA_ROWS + 2 * (NA_Q_TILES - 1)
NA_SHIFTS = NA_SHIFT0 // 2 + NA_KEY_BLOCKS


def _rpb_table_kernel(rpb_ref, o_ref):
    h, ei = pl.program_id(0), pl.program_id(1)
    e = 2 * ei - NA_SHIFT0
    sub = lax.broadcasted_iota(jnp.int32, (NA_BLOCK, NA_BLOCK), 0)
    lane = lax.broadcasted_iota(jnp.int32, (NA_BLOCK, NA_BLOCK), 1)
    qc, kc = sub % GRID_W, lane % GRID_W
    dr = e + lane // GRID_W - sub // GRID_W
    dc = kc - qc
    c0 = jnp.clip(qc - NA_COLS // 2, 0, GRID_W - NA_COLS)
    n_dc = 2 * NA_COLS - 1
    per_head = (2 * NA_ROWS - 1) * n_dc
    val = jnp.zeros((NA_BLOCK, NA_BLOCK), F32)
    for ddr in (-1, 0, 1):
        d = e + ddr
        base = h * per_head + (jnp.clip(d, 1 - NA_ROWS, NA_ROWS - 1) + NA_ROWS - 1) * n_dc

        def body(c, v, d=d, base=base):
            return jnp.where(jnp.where(dr == d, dc, n_dc) == c - (NA_COLS - 1), rpb_ref[base + c] * LOG2E, v)
        val = lax.fori_loop(0, n_dc, body, val)
    ok = jnp.where(kc >= c0, jnp.where(kc < c0 + NA_COLS, jnp.abs(dr), NA_ROWS), NA_ROWS) < NA_ROWS
    o_ref[0, 0] = jnp.where(ok, val, NEG_INF)


def _rpb_table_call(rpb):
    h = rpb.shape[0]
    return pl.pallas_call(
        _rpb_table_kernel, grid=(h, NA_SHIFTS),
        in_specs=[pl.BlockSpec(memory_space=pltpu.SMEM)],
        out_specs=pl.BlockSpec((1, 1, NA_BLOCK, NA_BLOCK), lambda hh, e: (hh, e, 0, 0)),
        out_shape=jax.ShapeDtypeStruct((h, NA_SHIFTS, NA_BLOCK, NA_BLOCK), F32),
        compiler_params=_params(("parallel", "parallel")), name="rpb_table",
    )(rpb.reshape(-1))


def _na_first_key_row(i, rows):
    return jnp.clip(2 * NA_Q_TILES * i - NA_ROWS // 2, 0, rows - 2 * NA_KEY_BLOCKS)


def _na_kernel(*refs, rows):
    q_ref = refs[0]
    k_refs = refs[1:1 + NA_KEY_BLOCKS]
    v_refs = refs[1 + NA_KEY_BLOCKS:1 + 2 * NA_KEY_BLOCKS]
    kc_ref, vc_ref, tab_ref, o_ref, s_ref, mask_ref, oh_ref = refs[1 + 2 * NA_KEY_BLOCKS:]
    i = pl.program_id(1)
    rs = _na_first_key_row(i, rows)
    row0 = 2 * NA_Q_TILES * i
    sub = lax.broadcasted_iota(jnp.int32, (NA_BLOCK, NA_BLOCK), 0)
    lane = lax.broadcasted_iota(jnp.int32, (NA_BLOCK, NA_BLOCK), 1)
    shift_idx = {}
    for ta in range(NA_Q_TILES):
        r0 = jnp.clip(row0 + 2 * ta + sub // GRID_W - NA_ROWS // 2, 0, rows - NA_ROWS)
        for tb in range(NA_KEY_BLOCKS):
            off = rs + 2 * tb + lane // GRID_W - r0
            mask_ref[ta, tb] = jnp.where(jnp.where(off >= 0, off, NA_ROWS) < NA_ROWS, 0.0, NEG_INF)
            shift_idx[ta, tb] = (rs - row0 + 2 * (tb - ta) + NA_SHIFT0) // 2
    n_ctx = kc_ref.shape[1]

    def scores(h):
        k_all = jnp.concatenate([r[h] for r in k_refs] + [kc_ref[h]], axis=0)
        bias = jnp.concatenate(
            [jnp.concatenate([tab_ref[h, shift_idx[ta, tb]] + mask_ref[ta, tb] for tb in range(NA_KEY_BLOCKS)]
                             + [jnp.zeros((NA_BLOCK, n_ctx), F32)], axis=1)
             for ta in range(NA_Q_TILES)], axis=0)
        return _dot_nt(q_ref[h], k_all) + bias

    def finish(h, s):
        v_all = jnp.concatenate([r[h] for r in v_refs] + [vc_ref[h]], axis=0)
        oh_ref[h] = _softmax2_pv(s, v_all)

    _head_pipeline(N_Q_HEADS, scores, finish, s_ref)
    o_ref[...] = jnp.concatenate([oh_ref[h] for h in range(N_Q_HEADS)], axis=-1).astype(o_ref.dtype)


def _na_call(q, k, v, kc, vc, table):
    nh, b, s, _ = q.shape
    c = kc.shape[2]
    rows = s // GRID_W
    blk = lambda t: pl.BlockSpec(
        (nh, None, NA_BLOCK, HEAD_DIM), lambda bi, i, t=t: (0, bi, _na_first_key_row(i, rows) // 2 + t, 0))
    ctx = pl.BlockSpec((nh, None, c, HEAD_DIM), lambda bi, i: (0, bi, 0, 0))
    tq = NA_Q_TILES * NA_BLOCK
    return pl.pallas_call(
        functools.partial(_na_kernel, rows=rows), grid=(b, s // tq),
        in_specs=([pl.BlockSpec((nh, None, tq, HEAD_DIM), lambda bi, i: (0, bi, i, 0))]
                  + [blk(t) for t in range(NA_KEY_BLOCKS)] * 2
                  + [ctx, ctx, pl.BlockSpec(table.shape, lambda bi, i: (0, 0, 0, 0))]),
        out_specs=pl.BlockSpec((None, tq, nh * HEAD_DIM), lambda bi, i: (bi, i, 0)),
        out_shape=jax.ShapeDtypeStruct((b, s, nh * HEAD_DIM), BF16),
        scratch_shapes=[pltpu.VMEM((2, tq, NA_KEY_BLOCKS * NA_BLOCK + c), F32),
                        pltpu.VMEM((NA_Q_TILES, NA_KEY_BLOCKS, NA_BLOCK, NA_BLOCK), F32),
                        pltpu.VMEM((nh, tq, HEAD_DIM), F32)],
        compiler_params=_params(("parallel", "arbitrary")), name="nbr_attn",
    )(q, *([k] * NA_KEY_BLOCKS), *([v] * NA_KEY_BLOCKS), kc, vc, table)


def _hyena_features(idx, n):
    bands = (HYENA_EMB - 1) // 2
    idx_f = idx.astype(F32)
    t = idx_f / (n - 1)
    w = 2.0 * math.pi * idx_f / n
    f = jnp.linspace(1e-4, bands - 1, bands, dtype=F32)[None, :]
    fw = f * w[:, None]
    z = jnp.concatenate([t[:, None], jnp.cos(fw), -jnp.sin(fw)], axis=-1)
    return jnp.pad(z, ((0, 0), (0, HYENA_HIDDEN - HYENA_EMB)))


def _filter_kernel(z_ref, w1_ref, b1_ref, w2_ref, b2_ref, fr_ref, w3_ref, dl_ref, taps_ref, ss_ref, *, tm, zero_tile):
    i = pl.program_id(0)
    zf = z_ref[...]
    fr = fr_ref[...]
    h1 = jnp.sin(fr * (_dot(zf, w1_ref[...], HIGHEST) + b1_ref[...]))
    h2 = jnp.sin(fr * (_dot(h1, w2_ref[...], HIGHEST) + b2_ref[...]))
    filt = _dot(h2.astype(BF16), w3_ref[...])
    decay = jnp.exp(-zf[:, 0:1] * dl_ref[...])
    row = lax.broadcasted_iota(jnp.int32, (tm, 1), 0)
    kill = jnp.where(i == zero_tile, 0, -1)

    @pl.when(i == 0)
    def _():
        ss_ref[...] = jnp.zeros_like(ss_ref)

    for o in range(HYENA_ORDER):
        tp = jnp.where(row == kill, 0.0, filt[:, o * HYENA_WIDTH:(o + 1) * HYENA_WIDTH] * decay)
        taps_ref[o] = tp.reshape(taps_ref.shape[1:])
        ss_ref[o] += jnp.sum(tp * tp, axis=0, keepdims=True)


def _filter_call(n, w1, b1, w2, b2, freq, w3, *, tm, t_layout):
    n_tiles = 2 * n // tm
    pos = jnp.arange(2 * n, dtype=jnp.int32)
    idx = jnp.where(pos < n, pos, 2 * n - pos)
    z = _hyena_features(idx, n)
    w1p = jnp.pad(w1, ((0, HYENA_HIDDEN - HYENA_EMB), (0, 0)))
    w3d = w3.reshape(HYENA_HIDDEN, HYENA_ORDER, 2, HYENA_WIDTH).transpose(2, 0, 1, 3).reshape(
        2, HYENA_HIDDEN, HYENA_ORDER * HYENA_WIDTH).astype(BF16)
    deltas = np.abs(np.linspace(math.log(HYENA_DECAY_TARGET) / HYENA_LONG_DECAY_PCT,
                                math.log(HYENA_DECAY_TARGET) / HYENA_SHORT_DECAY_PCT, HYENA_WIDTH))
    half = n_tiles // 2
    small = lambda a: pl.BlockSpec(a.shape, lambda i: (0,) * a.ndim)
    b1r, b2r, frr = b1.reshape(1, -1), b2.reshape(1, -1), freq.reshape(1, -1)
    dl = jnp.asarray(deltas.reshape(1, -1), F32)
    if t_layout:
        taps_shape = (HYENA_ORDER, tm // SUB, n_tiles * SUB, HYENA_WIDTH)
        taps_spec = pl.BlockSpec((HYENA_ORDER, tm // SUB, SUB, HYENA_WIDTH), lambda i: (0, 0, i, 0))
    else:
        taps_shape = (HYENA_ORDER, 2 * n, HYENA_WIDTH)
        taps_spec = pl.BlockSpec((HYENA_ORDER, tm, HYENA_WIDTH), lambda i: (0, i, 0))
    return pl.pallas_call(
        functools.partial(_filter_kernel, tm=tm, zero_tile=half), grid=(n_tiles,),
        in_specs=[pl.BlockSpec((tm, HYENA_HIDDEN), lambda i: (i, 0)), small(w1p), small(b1r), small(w2), small(b2r),
                  small(frr), pl.BlockSpec((None, HYENA_HIDDEN, HYENA_ORDER * HYENA_WIDTH), lambda i: (i // half, 0, 0)),
                  small(dl)],
        out_specs=[taps_spec, pl.BlockSpec((HYENA_ORDER, 1, HYENA_WIDTH), lambda i: (0, 0, 0))],
        out_shape=[jax.ShapeDtypeStruct(taps_shape, F32), jax.ShapeDtypeStruct((HYENA_ORDER, 1, HYENA_WIDTH), F32)],
        compiler_params=_params(("arbitrary",)), name="hyena_filter",
    )(z, w1p, b1r, w2, b2r, frr, w3d, dl)


def _dft_mats(n1, n2):
    n = n1 * n2
    a1 = 2 * np.pi * np.outer(np.arange(n1), np.arange(n1)) / n1
    c1, s1 = np.cos(a1), np.sin(a1)
    a2 = 2 * np.pi * np.outer(np.arange(n2), np.arange(n2)) / n2
    c2, s2 = np.cos(a2), np.sin(a2)
    h = n1 // 2
    mats = dict(
        sig=np.block([[c1[:, :h], s1[:, :h]], [-s1[:, :h], c1[:, :h]]]),
        filt=np.concatenate([c1, -s1], axis=0),
        fwd2=np.block([[c2, s2], [-s2, c2]]),
        inv2=np.block([[c2, -s2], [s2, c2]]),
        last=np.block([[c1[:h, :], -s1[:h, :]], [s1[:h, :], c1[:h, :]]]) / n,
    )
    mats = {k: jnp.asarray(v, F32).astype(BF16) for k, v in mats.items()}
    at = (jnp.arange(n1, dtype=jnp.int32)[:, None] * jnp.arange(n2, dtype=jnp.int32)[None, :]).astype(F32) * (2 * math.pi / n)
    tw = tuple(jnp.broadcast_to(f(at)[:, :, None], (n1, n2, LANES)) for f in (jnp.cos, jnp.sin))
    return mats, tw


def _stage1_kernel(m_ref, x_ref, o_ref, *, rows_in, rows_out, complex_in):
    for r in range(SUB):
        sl = pl.ds(r, rows_in, stride=SUB)
        if complex_in:
            x = jnp.concatenate([x_ref[0, sl, :], x_ref[1, sl, :]], axis=0)
        else:
            x = x_ref[sl, :]
        o_ref[pl.ds(r, rows_out, stride=SUB), :] = _dot(m_ref[...], x.astype(BF16))


def _stage1_call(mat, x, sel, *, complex_in, name):
    w = HYENA_WIDTH
    nl = w // LANES
    _, groups, rows8, _ = x.shape
    if complex_in:
        spec = pl.BlockSpec((2, None, rows8, LANES), lambda j, c: (0, j, 0, sel * nl + c))
    else:
        spec = pl.BlockSpec((None, None, rows8, LANES), lambda j, c: (sel, j, 0, c))
    rows_out = mat.shape[0]
    kern = functools.partial(_stage1_kernel, rows_in=rows8 // SUB, rows_out=rows_out, complex_in=complex_in)
    return pl.pallas_call(
        kern, grid=(groups, nl),
        in_specs=[pl.BlockSpec(mat.shape, lambda j, c: (0, 0)), spec],
        out_specs=pl.BlockSpec((None, rows_out * SUB, LANES), lambda j, c: (j, 0, c)),
        out_shape=jax.ShapeDtypeStruct((groups, rows_out * SUB, w), F32),
        compiler_params=_params(("parallel", "parallel")), name=name,
    )(mat, x)


def _lane_tile(t):
    return jnp.concatenate([t] * (HYENA_WIDTH // LANES), axis=-1)


def _twiddle_dft2(ar, ai, tc, ts, f_ref, n2):
    br = ar * tc + ai * ts
    bi = ai * tc - ar * ts
    x = _dot(f_ref[...], jnp.concatenate([br, bi], axis=0).astype(BF16))
    return x[:n2], x[n2:]


def _mid_kernel(f_ref, g_ref, ar_ref, ai_ref, fr_ref, fi_ref, tc_ref, ts_ref, ss_ref, or_ref, oi_ref, *, n2):
    tc, ts = _lane_tile(tc_ref[...]), _lane_tile(ts_ref[...])
    w = ar_ref.shape[-1]
    tc2, ts2 = jnp.concatenate([tc, tc], axis=1), jnp.concatenate([ts, ts], axis=1)
    ar = jnp.concatenate([ar_ref[...].reshape(n2, w), fr_ref[...].reshape(n2, w)], axis=1)
    ai = jnp.concatenate([ai_ref[...].reshape(n2, w), fi_ref[...].reshape(n2, w)], axis=1)
    xr, xi = _twiddle_dft2(ar, ai, tc2, ts2, f_ref, n2)
    norm = lax.rsqrt(ss_ref[...] + EPS)
    hr, hi = xr[:, w:] * norm, xi[:, w:] * norm
    xr, xi = xr[:, :w], xi[:, :w]
    yr = xr * hr - xi * hi
    yi = xr * hi + xi * hr
    y = _dot(g_ref[...], jnp.concatenate([yr, yi], axis=0).astype(BF16))
    cr, ci = y[:n2], y[n2:]
    or_ref[...] = (cr * tc - ci * ts).reshape(or_ref.shape)
    oi_ref[...] = (ci * tc + cr * ts).reshape(oi_ref.shape)


def _mid_call(fwd2, inv2, a, af, twc, tws, ss, order):
    groups, rows8, w = a.shape
    n2, n1 = groups * SUB, rows8 // (2 * SUB)
    col = lambda off: pl.BlockSpec((groups, SUB, w), lambda k: (0, k + off, 0))
    tw = pl.BlockSpec((None, n2, LANES), lambda k: (k, 0, 0))
    mat = pl.BlockSpec(fwd2.shape, lambda k: (0, 0))
    return pl.pallas_call(
        functools.partial(_mid_kernel, n2=n2), grid=(n1,),
        in_specs=[mat, mat, col(0), col(n1), col(0), col(n1), tw, tw,
                  pl.BlockSpec((None, 1, w), lambda k: (order, 0, 0))],
        out_specs=[col(0)] * 2,
        out_shape=[jax.ShapeDtypeStruct((groups, n1 * SUB, w), F32)] * 2,
        compiler_params=_params(("parallel",)), name="hyena_conv_mid",
    )(fwd2, inv2, a, a, af, af, twc, tws, ss)


def _last_kernel(m_ref, br_ref, bi_ref, z_ref, x_ref, skip_ref, o_ref, *, n1):
    skip = skip_ref[...]
    half = n1 // 2
    for r in range(SUB):
        full, part = pl.ds(r, n1, stride=SUB), pl.ds(r, half, stride=SUB)
        y = _dot(m_ref[...], jnp.concatenate([br_ref[full, :], bi_ref[full, :]], axis=0).astype(BF16))
        o_ref[0, part, :] = x_ref[0, part, :] * (y[:half] + z_ref[0, part, :] * skip)
        o_ref[1, part, :] = x_ref[1, part, :] * (y[half:] + z_ref[1, part, :] * skip)


def _last_call(mat, br, bi, zsrc, zcol, xsrc, xcol, skip, order):
    groups, rows8, w = br.shape
    n1 = rows8 // SUB
    nl = w // LANES
    b_spec = pl.BlockSpec((None, rows8, LANES), lambda j, c: (j, 0, c))
    seq = lambda col: pl.BlockSpec((2, None, rows8 // 2, LANES), lambda j, c: (0, j, 0, col * nl + c))
    return pl.pallas_call(
        functools.partial(_last_kernel, n1=n1), grid=(groups, nl),
        in_specs=[pl.BlockSpec(mat.shape, lambda j, c: (0, 0)), b_spec, b_spec, seq(zcol), seq(xcol),
                  pl.BlockSpec((None, 1, LANES), lambda j, c: (order, 0, c))],
        out_specs=seq(0),
        out_shape=jax.ShapeDtypeStruct((2, groups, rows8 // 2, w), F32),
        compiler_params=_params(("parallel", "parallel")), name="hyena_conv_last",
    )(mat, br, bi, zsrc, xsrc, skip)


def _hyena_long(uct, taps_t, ss, skip):
    n2 = uct.shape[1] * SUB
    mats, (twc, tws) = _dft_mats(FFT_N1, n2)
    skip3 = skip.reshape(HYENA_ORDER, 1, HYENA_WIDTH)
    z, zcol = uct, 0
    for o in range(HYENA_ORDER):
        af = _stage1_call(mats["filt"], taps_t, o, complex_in=False, name="hyena_filter_dft1")
        a = _stage1_call(mats["sig"], z, zcol, complex_in=True, name="hyena_conv_first")
        br, bi = _mid_call(mats["fwd2"], mats["inv2"], a, af, twc, tws, ss, o)
        z = _last_call(mats["last"], br, bi, z, zcol, uct, o + 1, skip3, o)
        zcol = 0
    return z


def _ctx_hyena_kernel(v_ref, x1_ref, x2_ref, taps_ref, ss_ref, skip_ref, fh_ref, fz_ref, fi_ref, o_ref, *, n):
    nn = 2 * n
    stack = lambda r: jnp.concatenate([r[0], r[1]], axis=0)
    z = stack(v_ref)
    for o, x_ref in enumerate((x1_ref, x2_ref)):
        h = _dot(fh_ref[...], taps_ref[o], HIGHEST) * lax.rsqrt(ss_ref[o] + EPS)
        hr, hi = h[:nn], h[nn:]
        zf = _dot(fz_ref[...], z, HIGHEST)
        zr, zi = zf[:nn], zf[nn:]
        y = _dot(fi_ref[...], jnp.concatenate([zr * hr - zi * hi, zr * hi + zi * hr], axis=0), HIGHEST)
        z = stack(x_ref) * (y + z * skip_ref[o])
    o_ref[0] = z[:n].astype(o_ref.dtype)
    o_ref[1] = z[n:].astype(o_ref.dtype)


def _ctx_hyena_call(u, taps, ss, skip):
    _, n, _ = u.shape
    nn = 2 * n
    ang = 2 * np.pi * np.outer(np.arange(nn), np.arange(nn)) / nn
    c, s = np.cos(ang), np.sin(ang)
    fh = jnp.asarray(np.concatenate([c, -s], axis=0), F32)
    fz = jnp.asarray(np.block([[c[:, :n], s[:, :n]], [-s[:, :n], c[:, :n]]]), F32)
    fi = jnp.asarray(np.block([[c[:n, :], -s[:n, :]], [s[:n, :], c[:n, :]]]) / nn, F32)
    nchunk = HYENA_WIDTH // LANES
    skip3 = skip.reshape(HYENA_ORDER, 1, HYENA_WIDTH)
    small = lambda a: pl.BlockSpec(a.shape, lambda j: (0,) * a.ndim)
    part = lambda g: pl.BlockSpec((2, n, LANES), lambda j, g=g: (0, 0, g * nchunk + j))

    return pl.pallas_call(
        functools.partial(_ctx_hyena_kernel, n=n), grid=(nchunk,),
        in_specs=[part(0), part(1), part(2),
                  pl.BlockSpec((HYENA_ORDER, nn, LANES), lambda j: (0, 0, j)),
                  pl.BlockSpec((HYENA_ORDER, 1, LANES), lambda j: (0, 0, j)),
                  pl.BlockSpec((HYENA_ORDER, 1, LANES), lambda j: (0, 0, j)),
                  small(fh), small(fz), small(fi)],
        out_specs=pl.BlockSpec((2, n, LANES), lambda j: (0, 0, j)),
        out_shape=jax.ShapeDtypeStruct((2, n, HYENA_WIDTH), BF16),
        compiler_params=_params(("parallel",)), name="hyena_ctx",
    )(u, u, u, taps, ss, skip3, fh, fz, fi)


def _outproj_kernel(x_ref, ya_ref, yb_ref, g_ref, w_ref, gate_ref, o_ref):
    tm, half = ya_ref.shape
    ya = (ya_ref[...].astype(F32) * g_ref[:, :half].astype(F32)).astype(BF16)
    yb = (yb_ref[...].reshape(tm, half).astype(F32) * g_ref[:, half:].astype(F32)).astype(BF16)
    upd = _dot(ya, w_ref[:half, :]) + _dot(yb, w_ref[half:, :])
    o_ref[...] = x_ref[...] + gate_ref[...] * upd


def _outproj_call(x2d, ya, yb, gates, w_bf, gate_vec, *, seq_len, tm, per_batch_mod, yb_t_layout=False):
    r, d = x2d.shape
    tps = seq_len // tm
    half = ya.shape[-1]
    if yb_t_layout:
        yb_spec = pl.BlockSpec((None, tm // SUB, SUB, half), lambda i: (i // tps, 0, i % tps, 0))
    else:
        yb_spec = pl.BlockSpec((tm, half), lambda i: (i, 0))
    mod_map = (lambda i: (i // tps, 0, 0)) if per_batch_mod else (lambda i: (0, 0, 0))
    return pl.pallas_call(
        _outproj_kernel, grid=(r // tm,),
        in_specs=[pl.BlockSpec((tm, d), lambda i: (i, 0)), pl.BlockSpec((tm, half), lambda i: (i, 0)), yb_spec,
                  pl.BlockSpec((tm, 2 * half), lambda i: (i, 0)), pl.BlockSpec(w_bf.shape, lambda i: (0, 0)),
                  pl.BlockSpec((None, 1, d), mod_map)],
        out_specs=pl.BlockSpec((tm, d), lambda i: (i, 0)),
        out_shape=jax.ShapeDtypeStruct((r, d), F32),
        compiler_params=_params(("parallel",)), name="outproj",
    )(x2d, ya, yb, gates, w_bf, gate_vec)


def _rope_tables(n_tokens):
    t = jnp.arange(n_tokens, dtype=jnp.int32)
    row = (t // GRID_W).astype(F32)
    col = (t % GRID_W).astype(F32)
    n_pairs = HEAD_DIM // 4
    inv = ROPE_THETA ** (-jnp.arange(n_pairs, dtype=F32) / n_pairs)
    ang = jnp.concatenate([row[:, None] * inv, col[:, None] * inv], axis=-1)
    cos = jnp.repeat(jnp.cos(ang), 2, axis=-1)
    sin = jnp.repeat(jnp.sin(ang), 2, axis=-1)
    even = (jnp.arange(HEAD_DIM) % 2 == 0)[None, :]
    tile = lambda a: jnp.tile(a, (1, LANES // HEAD_DIM))
    return tile(cos), tile(jnp.where(even, -sin, 0.0)), tile(jnp.where(even, 0.0, sin))


def _gain_rows(*gains):
    rows = [jnp.tile(g, LANES // HEAD_DIM) for g in gains]
    rows += [jnp.zeros((LANES,), F32)] * (8 - len(rows))
    return jnp.stack(rows)


def kernel(x, c, ctx, c_ctx, norm_g, w_ada, b_ada, w_in, w_out, glob_q_norm, glob_k_norm, hy_conv_w, hy_conv_b, hy_w1, hy_b1, hy_w2, hy_b2, hy_freq, hy_w3, hy_skip, win_q_norm, win_k_norm, win_sink, nat_q_norm, nat_k_norm, nat_rpb):
    b, s, d = x.shape
    n_ctx = ctx.shape[1]
    n2 = 2 * s // FFT_N1
    tm = n2
    w_in_bf, w_out_bf = w_in.astype(BF16), w_out.astype(BF16)

    cond = jnp.zeros((8, d), F32).at[:b].set(c).at[b].set(c_ctx)
    mod = _ada_call(cond, w_ada, b_ada)
    shift, scale, gate = (mod[:, :, i * d:(i + 1) * d] for i in range(3))
    lat = lambda m, l: m[l, :b].reshape(b, 1, d)
    cx = lambda m, l: m[l, b:b + 1].reshape(1, 1, d)

    rope = _rope_tables(s)
    x2 = x.reshape(b * s, d)
    xc2 = ctx.reshape(b * n_ctx, d)
    heads4 = lambda a, n: a.reshape(a.shape[0], b, n, a.shape[-1])

    gains0 = _gain_rows(glob_q_norm[0], glob_k_norm[0])
    gw, kw = GROUP_WIDTH, N_KV_HEADS * HEAD_DIM
    hyw = (HYENA_ORDER + 1) * HYENA_WIDTH
    plan_lat = (("qk", 0, gw, 0, True, QK_SCALE2), ("qk", gw, kw, 1, True, 1.0), ("vones", gw + kw, kw),
                ("hy", gw + 2 * kw, hyw), ("gate", gw + 2 * kw + hyw, 2 * gw))
    plan_ctx = (("qk", 0, gw, 0, False, QK_SCALE2), ("qk", gw, kw, 1, False, 1.0), ("vones", gw + kw, kw),
                ("hy", gw + 2 * kw, hyw), ("gate", gw + 2 * kw + hyw, 2 * gw))
    conv = (hy_conv_w[0], hy_conv_b[0])
    q0, k0, v0, uct, g0 = _inproj_call(
        x2, lat(shift, 0), lat(scale, 0), norm_g[0], w_in_bf[0], plan_lat, seq_len=s, tm=tm, per_batch_mod=True,
        rope_tabs=rope, gains=gains0, conv=conv, hy_t_layout=True, name="inproj_even")
    qc, kc, vc, ucc, gc = _inproj_call(
        xc2, cx(shift, 0), cx(scale, 0), norm_g[0], w_in_bf[0], plan_ctx, seq_len=n_ctx, tm=n_ctx,
        per_batch_mod=False, gains=gains0, conv=conv, name="inproj_even_ctx")
    kc4, vc4 = heads4(kc, n_ctx), heads4(vc, n_ctx)
    k_all = jnp.concatenate([kc4, heads4(k0, s)], axis=2)
    v_all = jnp.concatenate([vc4, heads4(v0, s)], axis=2)
    ya = _flash_call(heads4(q0, s), k_all, v_all, tq=256, name="global_gqa")
    yca = _flash_call(heads4(qc, n_ctx), kc4, vc4, tq=n_ctx, name="ctx_gqa")

    filt = (hy_w1[0], hy_b1[0], hy_w2[0], hy_b2[0], hy_freq[0], hy_w3[0])
    taps_t, ss = _filter_call(s, *filt, tm=n2, t_layout=True)
    ybt = _hyena_long(uct, taps_t, ss, hy_skip[0])
    taps_c, ss_c = _filter_call(n_ctx, *filt, tm=n_ctx, t_layout=False)
    ycb = _ctx_hyena_call(ucc.reshape(b, n_ctx, hyw), taps_c, ss_c, hy_skip[0])

    x2 = _outproj_call(x2, ya.reshape(b * s, gw), ybt, g0, w_out_bf[0], lat(gate, 0), seq_len=s, tm=tm,
                       per_batch_mod=True, yb_t_layout=True)
    xc2 = _outproj_call(xc2, yca.reshape(b * n_ctx, gw), ycb.reshape(b * n_ctx, gw), gc, w_out_bf[0], cx(gate, 0),
                        seq_len=n_ctx, tm=n_ctx, per_batch_mod=False)

    gains1 = _gain_rows(win_q_norm[0], win_k_norm[0], nat_q_norm[0], nat_k_norm[0])
    o_nq = gw + 2 * kw
    plan_lat = (("qk", 0, gw, 0, True, QK_SCALE2), ("qk", gw, kw, 1, True, 1.0), ("v", gw + kw, kw),
                ("qk", o_nq, gw, 2, False, QK_SCALE2), ("qk", o_nq + gw, gw, 3, False, 1.0), ("v", o_nq + 2 * gw, gw),
                ("gate", o_nq + 3 * gw, 2 * gw))
    plan_ctx = (("qk", gw, kw, 1, False, 1.0), ("v", gw + kw, kw),
                ("qk", o_nq + gw, gw, 3, False, 1.0), ("v", o_nq + 2 * gw, gw))
    qw, kwl, vwl, qn, kn, vn, g1 = _inproj_call(
        x2, lat(shift, 1), lat(scale, 1), norm_g[1], w_in_bf[1], plan_lat, seq_len=s, tm=tm, per_batch_mod=True,
        rope_tabs=rope, gains=gains1, name="inproj_odd")
    kwc, vwc, knc, vnc = _inproj_call(
        xc2, cx(shift, 1), cx(scale, 1), norm_g[1], w_in_bf[1], plan_ctx, seq_len=n_ctx, tm=n_ctx,
        per_batch_mod=False, gains=gains1, name="inproj_odd_ctx")
    yw = _window_call(win_sink[0], heads4(qw, s), heads4(kwc, n_ctx), heads4(vwc, n_ctx), heads4(kwl, s), heads4(vwl, s))
    table = _rpb_table_call(nat_rpb[0])
    yn = _na_call(heads4(qn, s), heads4(kn, s), heads4(vn, s), heads4(knc, n_ctx), heads4(vnc, n_ctx), table)
    x2 = _outproj_call(x2, yw.reshape(b * s, gw), yn.reshape(b * s, gw), g1, w_out_bf[1], lat(gate, 1), seq_len=s,
                       tm=tm, per_batch_mod=True)
    return x2.reshape(b, s, d)
```

```python
import functools
import math

import numpy as np
import jax
import jax.numpy as jnp
from jax import lax
from jax.experimental import pallas as pl
from jax.experimental.pallas import tpu as pltpu

F32 = jnp.float32
BF16 = jnp.bfloat16
HIGHEST = lax.Precision.HIGHEST

GRID_W = 64
HEAD_DIM = 64
GROUP_WIDTH = 512
N_Q_HEADS = 8
N_KV_HEADS = 2
GQA_GROUP = N_Q_HEADS // N_KV_HEADS
HYENA_WIDTH = 512
HYENA_ORDER = 2
HYENA_EMB = 33
HYENA_HIDDEN = 64
HYENA_DECAY_TARGET = 1e-2
HYENA_SHORT_DECAY_PCT = 0.3
HYENA_LONG_DECAY_PCT = 1.5
WINDOW = 128
NA_ROWS = 8
NA_COLS = 16
ROPE_THETA = 10000.0
EPS = 1e-6
NEG_INF = -1e30
LOG2E = math.log2(math.e)
QK_SCALE2 = HEAD_DIM ** -0.5 * LOG2E

LANES = 128
SUB = 8
HALO = 16
FFT_N1 = 128
FFT_GROUPS = 4
MXU_COLS = 256
FLASH_MAX_CHUNK = 1280
FLASH_UNROLL = 4
VMEM_LIMIT = 56 * 1024 * 1024


def _dot(a, b, precision=None):
    return jnp.dot(a, b, preferred_element_type=F32, precision=precision)


def _dot_nt(a, b):
    return lax.dot_general(a, b, (((1,), (1,)), ((), ())), preferred_element_type=F32)


def _params(sem):
    return pltpu.CompilerParams(dimension_semantics=sem, vmem_limit_bytes=VMEM_LIMIT)


def _silu(v):
    return v / (1.0 + jnp.exp(-v))


def _softmax2_pv(s, v, sink=None):
    m = jnp.max(s, axis=1, keepdims=True)
    if sink is not None:
        m = jnp.maximum(m, sink)
    p = jnp.exp2(s - m)
    l = jnp.sum(p, axis=1, keepdims=True)
    if sink is not None:
        l = l + jnp.exp2(sink - m)
    return _dot(p.astype(BF16), v) / l


def _ada_kernel(cond_ref, w_ref, b_ref, o_ref):
    o_ref[0] = _dot(_silu(cond_ref[...]), w_ref[0], HIGHEST) + b_ref[0]


def _ada_call(cond, w_ada, b_ada):
    depth, d, d3 = w_ada.shape
    tn = 1024
    return pl.pallas_call(
        _ada_kernel,
        grid=(depth, d3 // tn),
        in_specs=[pl.BlockSpec((8, d), lambda l, j: (0, 0)),
                  pl.BlockSpec((1, d, tn), lambda l, j: (l, 0, j)),
                  pl.BlockSpec((1, 1, tn), lambda l, j: (l, 0, j))],
        out_specs=pl.BlockSpec((1, 8, tn), lambda l, j: (l, 0, j)),
        out_shape=jax.ShapeDtypeStruct((depth, 8, d3), F32),
        compiler_params=_params(("parallel", "parallel")),
        name="adaln",
    )(cond, w_ada, b_ada.reshape(depth, 1, d3))


def _inproj_kernel(*refs, plan, tm, tps, has_halo, has_rope):
    it = iter(refs)
    x_ref = next(it)
    if has_halo:
        xp_ref, xn_ref = next(it), next(it)
    sh_ref, sc_ref, g_ref, w_ref = next(it), next(it), next(it), next(it)
    if has_rope:
        cos_ref, s1_ref, s2_ref = next(it), next(it), next(it)
    gains_ref, mavg_ref = next(it), next(it)
    if has_halo:
        cw_ref, cb_ref = next(it), next(it)
    outs = [next(it) for _ in plan]
    if has_halo:
        xe_ref, hu_ref = next(it), next(it)

    g, sh, sc = g_ref[...], sh_ref[...], sc_ref[...]

    def modnorm(xv):
        ms = jnp.mean(xv * xv, axis=-1, keepdims=True)
        return (xv * lax.rsqrt(ms + EPS) * g) * (1.0 + sc) + sh

    h = modnorm(x_ref[...]).astype(BF16)
    mavg = mavg_ref[...]

    for seg, o_ref in zip(plan, outs):
        kind, start, width = seg[0], seg[1], seg[2]
        if kind == "hy":
            xe_ref[0:HALO] = modnorm(xp_ref[...]).astype(BF16)
            xe_ref[HALO:HALO + tm] = h
            xe_ref[HALO + tm:HALO + tm + HALO] = modnorm(xn_ref[...]).astype(BF16)
            hu_ref[...] = _dot(xe_ref[...], w_ref[:, start:start + width])
            pos = pl.program_id(0) % tps
            row = lax.broadcasted_iota(jnp.int32, (tm, 1), 0)
            kill_prev = jnp.where(pos == 0, 0, -1)
            kill_next = jnp.where(pos == tps - 1, tm - 1, -1)
            a = jnp.where(row == kill_prev, 0.0, hu_ref[pl.ds(HALO - 1, tm), :])
            b = hu_ref[pl.ds(HALO, tm), :]
            c = jnp.where(row == kill_next, 0.0, hu_ref[pl.ds(HALO + 1, tm), :])
            uc = a * cw_ref[0:1, :] + b * cw_ref[1:2, :] + c * cw_ref[2:3, :] + cb_ref[...]
            o_ref[...] = uc.reshape(o_ref.shape)
            continue
        pv = _dot(h, w_ref[:, start:start + width])
        if kind == "gate":
            o_ref[...] = _silu(pv).astype(o_ref.dtype)
        elif kind == "v":
            for hd in range(width // HEAD_DIM):
                o_ref[hd] = pv[:, hd * HEAD_DIM:(hd + 1) * HEAD_DIM].astype(o_ref.dtype)
        elif kind == "vones":
            lane = lax.broadcasted_iota(jnp.int32, (tm, LANES - HEAD_DIM), 1)
            tail = jnp.where(lane == 0, 1.0, 0.0)
            for hd in range(width // HEAD_DIM):
                o_ref[hd] = jnp.concatenate([pv[:, hd * HEAD_DIM:(hd + 1) * HEAD_DIM], tail], axis=1).astype(o_ref.dtype)
        else:
            gain_row, rope, scale = seg[3], seg[4], seg[5]
            gain = gains_ref[gain_row:gain_row + 1, :]
            for c in range(width // LANES):
                v = pv[:, c * LANES:(c + 1) * LANES]
                sq = v * v
                hi = sq.astype(BF16)
                lo = (sq - hi.astype(F32)).astype(BF16)
                ms = _dot(jnp.concatenate([hi, lo], axis=1), mavg)
                vn = v * lax.rsqrt(ms + EPS) * gain
                if rope:
                    vn = (vn * cos_ref[...] + pltpu.roll(vn, LANES - 1, 1) * s1_ref[...]
                          + pltpu.roll(vn, 1, 1) * s2_ref[...])
                vn = (vn * scale).astype(o_ref.dtype)
                o_ref[2 * c] = vn[:, :HEAD_DIM]
                o_ref[2 * c + 1] = vn[:, HEAD_DIM:]


def _inproj_call(x2d, shift, scale, norm_g, w_bf, plan, *, seq_len, tm, per_batch_mod,
                 rope_tabs=None, gains=None, conv=None, hy_t_layout=False, name="inproj"):
    r, d = x2d.shape
    tps = seq_len // tm
    n_tiles = r // tm
    n_seq = r // seq_len
    has_halo = any(s[0] == "hy" for s in plan)
    has_rope = rope_tabs is not None
    hb = tm // HALO
    mod_map = (lambda i: (i // tps, 0, 0)) if per_batch_mod else (lambda i: (0, 0, 0))

    args, specs = [x2d], [pl.BlockSpec((tm, d), lambda i: (i, 0))]
    if has_halo:
        args += [x2d, x2d]
        specs += [pl.BlockSpec((HALO, d), lambda i: (jnp.maximum(i * hb - 1, 0), 0)),
                  pl.BlockSpec((HALO, d), lambda i: (jnp.minimum((i + 1) * hb, r // HALO - 1), 0))]
    args += [shift, scale, norm_g.reshape(1, d), w_bf]
    specs += [pl.BlockSpec((None, 1, d), mod_map), pl.BlockSpec((None, 1, d), mod_map),
              pl.BlockSpec((1, d), lambda i: (0, 0)),
              pl.BlockSpec(w_bf.shape, lambda i: (0, 0))]
    if has_rope:
        args += list(rope_tabs)
        specs += [pl.BlockSpec((tm, LANES), lambda i: (i % tps, 0))] * 3
    mavg = np.kron(np.eye(LANES // HEAD_DIM), np.full((HEAD_DIM, HEAD_DIM), 1.0 / HEAD_DIM))
    args += [gains, jnp.asarray(np.concatenate([mavg, mavg], axis=0), BF16)]
    specs += [pl.BlockSpec(gains.shape, lambda i: (0, 0)), pl.BlockSpec((2 * LANES, LANES), lambda i: (0, 0))]
    if has_halo:
        cw, cb = conv
        args += [cw, cb.reshape(1, -1)]
        specs += [pl.BlockSpec(cw.shape, lambda i: (0, 0)), pl.BlockSpec((1, cb.shape[-1]), lambda i: (0, 0))]

    out_shapes, out_specs = [], []
    for seg in plan:
        kind, width = seg[0], seg[2]
        if kind == "hy":
            if hy_t_layout:
                out_shapes.append(jax.ShapeDtypeStruct((n_seq, tm // SUB, tps * SUB, width), F32))
                out_specs.append(pl.BlockSpec((None, tm // SUB, SUB, width), lambda i: (i // tps, 0, i % tps, 0)))
            else:
                out_shapes.append(jax.ShapeDtypeStruct((r, width), F32))
                out_specs.append(pl.BlockSpec((tm, width), lambda i: (i, 0)))
        elif kind == "gate":
            out_shapes.append(jax.ShapeDtypeStruct((r, width), BF16))
            out_specs.append(pl.BlockSpec((tm, width), lambda i: (i, 0)))
        else:
            nh = width // HEAD_DIM
            hw = LANES if kind == "vones" else HEAD_DIM
            out_shapes.append(jax.ShapeDtypeStruct((nh, r, hw), BF16))
            out_specs.append(pl.BlockSpec((nh, tm, hw), lambda i: (0, i, 0)))
    scratch = []
    if has_halo:
        hw = [s[2] for s in plan if s[0] == "hy"][0]
        scratch = [pltpu.VMEM((tm + 2 * HALO, d), BF16), pltpu.VMEM((tm + 2 * HALO, hw), F32)]
    kern = functools.partial(_inproj_kernel, plan=plan, tm=tm, tps=tps, has_halo=has_halo, has_rope=has_rope)
    return pl.pallas_call(
        kern, grid=(n_tiles,), in_specs=specs, out_specs=out_specs, out_shape=out_shapes,
        scratch_shapes=scratch, compiler_params=_params(("parallel",)), name=name,
    )(*args)


def _flash_kernel(q_ref, k_ref, v_ref, o_ref, m_ref, acc_ref, s_ref, *, tq, tk, n_chunks, unroll):
    rows = GQA_GROUP * tq
    q = q_ref[...].reshape(rows, HEAD_DIM)
    m_ref[...] = jnp.full((rows, LANES), NEG_INF, F32)
    acc_ref[...] = jnp.zeros((rows, LANES), F32)

    def scores(j):
        return _dot_nt(q, k_ref[pl.ds(pl.multiple_of(j * tk, tk), tk), :])

    def softmax_pv(s, j):
        m_prev = m_ref[...]
        m_new = jnp.maximum(m_prev, jnp.max(s, axis=1, keepdims=True))
        m_ref[...] = m_new
        p = jnp.exp2(s - jnp.concatenate([m_new] * (tk // LANES), axis=1)).astype(BF16)
        pv = _dot(p, v_ref[pl.ds(pl.multiple_of(j * tk, tk), tk), :])
        acc_ref[...] = jnp.exp2(m_prev - m_new) * acc_ref[...] + pv

    s_ref[0] = scores(0)
    if n_chunks > 1:
        def half(j, cur):
            s_ref[1 - cur] = scores(j + 1)
            softmax_pv(s_ref[cur], j)

        def body(i, carry):
            for u in range(unroll):
                half(unroll * i + u, u % 2)
            return carry
        lax.fori_loop(0, (n_chunks - 1) // unroll, body, 0)
    softmax_pv(s_ref[(n_chunks - 1) % 2], n_chunks - 1)
    acc = acc_ref[...]
    o = acc[:, :HEAD_DIM] / acc[:, HEAD_DIM:HEAD_DIM + 1]
    o_ref[...] = jnp.concatenate([o[gi * tq:(gi + 1) * tq] for gi in range(GQA_GROUP)], axis=-1).astype(o_ref.dtype)


def _flash_chunk(n_keys):
    for tk in range(FLASH_MAX_CHUNK, 0, -MXU_COLS):
        if n_keys % tk == 0 and (n_keys // tk == 1 or (n_keys // tk - 1) % 2 == 0):
            return tk
    raise ValueError(f"no key chunk for {n_keys} keys")


def _flash_call(q, k, v, *, tq, name="gqa"):
    _, b, sq, _ = q.shape
    n = k.shape[2]
    tk = _flash_chunk(n)
    n_chunks = n // tk
    unroll = math.gcd(FLASH_UNROLL, n_chunks - 1) if n_chunks > 1 else 0
    kv_spec = lambda w: pl.BlockSpec((None, None, n, w), lambda bi, j, i: (j, bi, 0, 0))
    rows = GQA_GROUP * tq
    kern = functools.partial(_flash_kernel, tq=tq, tk=tk, n_chunks=n_chunks, unroll=unroll)
    return pl.pallas_call(
        kern, grid=(b, N_KV_HEADS, sq // tq),
        in_specs=[pl.BlockSpec((GQA_GROUP, None, tq, HEAD_DIM), lambda bi, j, i: (j, bi, i, 0)),
                  kv_spec(HEAD_DIM), kv_spec(LANES)],
        out_specs=pl.BlockSpec((None, tq, GQA_GROUP * HEAD_DIM), lambda bi, j, i: (bi, i, j)),
        out_shape=jax.ShapeDtypeStruct((b, sq, GROUP_WIDTH), BF16),
        scratch_shapes=[pltpu.VMEM((rows, LANES), F32), pltpu.VMEM((rows, LANES), F32),
                        pltpu.VMEM((2 if n_chunks > 1 else 1, rows, tk), F32)],
        compiler_params=_params(("parallel", "parallel", "arbitrary")), name=name,
    )(q, k, v)


def _head_pipeline(n_heads, scores_fn, finish_fn, s_ref):
    s_ref[0] = scores_fn(0)
    for h in range(n_heads - 1):
        s_ref[(h + 1) % 2] = scores_fn(h + 1)
        finish_fn(h, s_ref[h % 2])
    finish_fn(n_heads - 1, s_ref[(n_heads - 1) % 2])


def _window_kernel(sink_ref, q_ref, kc_ref, vc_ref, kl_ref, vl_ref, o_ref, s_ref, mask_ref, oh_ref, *, tq, span, seq):
    j, i = pl.program_id(1), pl.program_id(2)
    start = pl.multiple_of(jnp.clip(i * tq - WINDOW, 0, seq - span), LANES)
    n_ctx = kc_ref.shape[0]
    qpos = i * tq + lax.broadcasted_iota(jnp.int32, (tq, span), 0)
    kpos = start + lax.broadcasted_iota(jnp.int32, (tq, span), 1)
    mask_ref[...] = jnp.concatenate([jnp.where(jnp.abs(qpos - kpos) <= WINDOW, 0.0, NEG_INF),
                                     jnp.zeros((tq, n_ctx), F32)], axis=1)

    def scores(g):
        k_all = jnp.concatenate([kl_ref[pl.ds(start, span), :], kc_ref[...]], axis=0)
        return _dot_nt(q_ref[g], k_all) + mask_ref[...]

    def finish(g, s):
        v_all = jnp.concatenate([vl_ref[pl.ds(start, span), :], vc_ref[...]], axis=0)
        oh_ref[g] = _softmax2_pv(s, v_all, sink_ref[j * GQA_GROUP + g] * LOG2E)

    _head_pipeline(GQA_GROUP, scores, finish, s_ref)
    o_ref[...] = jnp.concatenate([oh_ref[g] for g in range(GQA_GROUP)], axis=-1).astype(o_ref.dtype)


def _window_call(sink, q, kc, vc, kl, vl, *, tq=256):
    _, b, s, _ = q.shape
    c = kc.shape[2]
    span = tq + 2 * WINDOW
    kern = functools.partial(_window_kernel, tq=tq, span=span, seq=s)
    full = lambda n: pl.BlockSpec((None, None, n, HEAD_DIM), lambda bi, j, i: (j, bi, 0, 0))
    return pl.pallas_call(
        kern, grid=(b, N_KV_HEADS, s // tq),
        in_specs=[pl.BlockSpec(memory_space=pltpu.SMEM),
                  pl.BlockSpec((GQA_GROUP, None, tq, HEAD_DIM), lambda bi, j, i: (j, bi, i, 0)),
                  full(c), full(c), full(s), full(s)],
        out_specs=pl.BlockSpec((None, tq, GQA_GROUP * HEAD_DIM), lambda bi, j, i: (bi, i, j)),
        out_shape=jax.ShapeDtypeStruct((b, s, GROUP_WIDTH), BF16),
        scratch_shapes=[pltpu.VMEM((2, tq, span + c), F32), pltpu.VMEM((tq, span + c), F32),
                        pltpu.VMEM((GQA_GROUP, tq, HEAD_DIM), F32)],
        compiler_params=_params(("parallel", "parallel", "arbitrary")), name="window_gqa",
    )(sink, q, kc, vc, kl, vl)


NA_BLOCK = 2 * GRID_W
NA_Q_TILES = 2
NA_KEY_BLOCKS = NA_Q_TILES + NA_ROWS // 2
NA_SHIFT0 = NA_ROWS + 2 * (NA_Q_TILES - 1)
NA_SHIFTS = NA_SHIFT0 // 2 + NA_KEY_BLOCKS


def _rpb_table_kernel(rpb_ref, o_ref):
    h, ei = pl.program_id(0), pl.program_id(1)
    e = 2 * ei - NA_SHIFT0
    sub = lax.broadcasted_iota(jnp.int32, (NA_BLOCK, NA_BLOCK), 0)
    lane = lax.broadcasted_iota(jnp.int32, (NA_BLOCK, NA_BLOCK), 1)
    qc, kc = sub % GRID_W, lane % GRID_W
    dr = e + lane // GRID_W - sub // GRID_W
    dc = kc - qc
    c0 = jnp.clip(qc - NA_COLS // 2, 0, GRID_W - NA_COLS)
    n_dc = 2 * NA_COLS - 1
    per_head = (2 * NA_ROWS - 1) * n_dc
    flat = (dr - (e - 1)) * n_dc + dc + (NA_COLS - 1)
    base = h * per_head + (e - 1 + NA_ROWS - 1) * n_dc
    last = rpb_ref.shape[0] - 1

    def body(c, v):
        return jnp.where(flat == c, rpb_ref[jnp.clip(base + c, 0, last)] * LOG2E, v)
    val = lax.fori_loop(0, 3 * n_dc, body, jnp.zeros((NA_BLOCK, NA_BLOCK), F32), unroll=n_dc)
    ok = jnp.where(kc >= c0, jnp.where(kc < c0 + NA_COLS, jnp.abs(dr), NA_ROWS), NA_ROWS) < NA_ROWS
    o_ref[0, 0] = jnp.where(ok, val, NEG_INF)


def _rpb_table_call(rpb):
    h = rpb.shape[0]
    return pl.pallas_call(
        _rpb_table_kernel, grid=(h, NA_SHIFTS),
        in_specs=[pl.BlockSpec(memory_space=pltpu.SMEM)],
        out_specs=pl.BlockSpec((1, 1, NA_BLOCK, NA_BLOCK), lambda hh, e: (hh, e, 0, 0)),
        out_shape=jax.ShapeDtypeStruct((h, NA_SHIFTS, NA_BLOCK, NA_BLOCK), F32),
        compiler_params=_params(("parallel", "parallel")), name="rpb_table",
    )(rpb.reshape(-1))


def _na_first_key_row(i, rows):
    return jnp.clip(2 * NA_Q_TILES * i - NA_ROWS // 2, 0, rows - 2 * NA_KEY_BLOCKS)


def _na_kernel(*refs, rows):
    q_ref = refs[0]
    k_refs = refs[1:1 + NA_KEY_BLOCKS]
    v_refs = refs[1 + NA_KEY_BLOCKS:1 + 2 * NA_KEY_BLOCKS]
    kc_ref, vc_ref, tab_ref, o_ref, s_ref, mask_ref, oh_ref = refs[1 + 2 * NA_KEY_BLOCKS:]
    i = pl.program_id(1)
    rs = _na_first_key_row(i, rows)
    row0 = 2 * NA_Q_TILES * i
    sub = lax.broadcasted_iota(jnp.int32, (NA_BLOCK, NA_BLOCK), 0)
    lane = lax.broadcasted_iota(jnp.int32, (NA_BLOCK, NA_BLOCK), 1)
    shift_idx = {}
    for ta in range(NA_Q_TILES):
        r0 = jnp.clip(row0 + 2 * ta + sub // GRID_W - NA_ROWS // 2, 0, rows - NA_ROWS)
        for tb in range(NA_KEY_BLOCKS):
            off = rs + 2 * tb + lane // GRID_W - r0
            mask_ref[ta, tb] = jnp.where(jnp.where(off >= 0, off, NA_ROWS) < NA_ROWS, 0.0, NEG_INF)
            shift_idx[ta, tb] = (rs - row0 + 2 * (tb - ta) + NA_SHIFT0) // 2
    n_ctx = kc_ref.shape[1]

    def scores(h):
        k_all = jnp.concatenate([r[h] for r in k_refs] + [kc_ref[h]], axis=0)
        bias = jnp.concatenate(
            [jnp.concatenate([tab_ref[h, shift_idx[ta, tb]] + mask_ref[ta, tb] for tb in range(NA_KEY_BLOCKS)]
                             + [jnp.zeros((NA_BLOCK, n_ctx), F32)], axis=1)
             for ta in range(NA_Q_TILES)], axis=0)
        return _dot_nt(q_ref[h], k_all) + bias

    def finish(h, s):
        v_all = jnp.concatenate([r[h] for r in v_refs] + [vc_ref[h]], axis=0)
        oh_ref[h] = _softmax2_pv(s, v_all)

    _head_pipeline(N_Q_HEADS, scores, finish, s_ref)
    o_ref[...] = jnp.concatenate([oh_ref[h] for h in range(N_Q_HEADS)], axis=-1).astype(o_ref.dtype)


def _na_call(q, k, v, kc, vc, table):
    nh, b, s, _ = q.shape
    c = kc.shape[2]
    rows = s // GRID_W
    blk = lambda t: pl.BlockSpec(
        (nh, None, NA_BLOCK, HEAD_DIM), lambda bi, i, t=t: (0, bi, _na_first_key_row(i, rows) // 2 + t, 0))
    ctx = pl.BlockSpec((nh, None, c, HEAD_DIM), lambda bi, i: (0, bi, 0, 0))
    tq = NA_Q_TILES * NA_BLOCK
    return pl.pallas_call(
        functools.partial(_na_kernel, rows=rows), grid=(b, s // tq),
        in_specs=([pl.BlockSpec((nh, None, tq, HEAD_DIM), lambda bi, i: (0, bi, i, 0))]
                  + [blk(t) for t in range(NA_KEY_BLOCKS)] * 2
                  + [ctx, ctx, pl.BlockSpec(table.shape, lambda bi, i: (0, 0, 0, 0))]),
        out_specs=pl.BlockSpec((None, tq, nh * HEAD_DIM), lambda bi, i: (bi, i, 0)),
        out_shape=jax.ShapeDtypeStruct((b, s, nh * HEAD_DIM), BF16),
        scratch_shapes=[pltpu.VMEM((2, tq, NA_KEY_BLOCKS * NA_BLOCK + c), F32),
                        pltpu.VMEM((NA_Q_TILES, NA_KEY_BLOCKS, NA_BLOCK, NA_BLOCK), F32),
                        pltpu.VMEM((nh, tq, HEAD_DIM), F32)],
        compiler_params=_params(("parallel", "arbitrary")), name="nbr_attn",
    )(q, *([k] * NA_KEY_BLOCKS), *([v] * NA_KEY_BLOCKS), kc, vc, table)


def _hyena_features(idx, n):
    bands = (HYENA_EMB - 1) // 2
    idx_f = idx.astype(F32)
    t = idx_f / (n - 1)
    w = 2.0 * math.pi * idx_f / n
    f = jnp.linspace(1e-4, bands - 1, bands, dtype=F32)[None, :]
    fw = f * w[:, None]
    z = jnp.concatenate([t[:, None], jnp.cos(fw), -jnp.sin(fw)], axis=-1)
    return jnp.pad(z, ((0, 0), (0, HYENA_HIDDEN - HYENA_EMB)))


def _filter_kernel(z_ref, w1_ref, b1_ref, w2_ref, b2_ref, fr_ref, w3_ref, dl_ref, taps_ref, ss_ref, *, tm, zero_tile):
    i = pl.program_id(0)
    zf = z_ref[...]
    fr = fr_ref[...]
    h1 = jnp.sin(fr * (_dot(zf, w1_ref[...], HIGHEST) + b1_ref[...]))
    h2 = jnp.sin(fr * (_dot(h1, w2_ref[...], HIGHEST) + b2_ref[...]))
    filt = _dot(h2.astype(BF16), w3_ref[...])
    decay = jnp.exp(-zf[:, 0:1] * dl_ref[...])
    row = lax.broadcasted_iota(jnp.int32, (tm, 1), 0)
    kill = jnp.where(i == zero_tile, 0, -1)

    @pl.when(i == 0)
    def _():
        ss_ref[...] = jnp.zeros_like(ss_ref)

    for o in range(HYENA_ORDER):
        tp = jnp.where(row == kill, 0.0, filt[:, o * HYENA_WIDTH:(o + 1) * HYENA_WIDTH] * decay)
        taps_ref[o] = tp.reshape(taps_ref.shape[1:])
        ss_ref[o] += jnp.sum(tp * tp, axis=0, keepdims=True)


def _filter_call(n, w1, b1, w2, b2, freq, w3, *, tm, t_layout):
    n_tiles = 2 * n // tm
    pos = jnp.arange(2 * n, dtype=jnp.int32)
    idx = jnp.where(pos < n, pos, 2 * n - pos)
    z = _hyena_features(idx, n)
    w1p = jnp.pad(w1, ((0, HYENA_HIDDEN - HYENA_EMB), (0, 0)))
    w3d = w3.reshape(HYENA_HIDDEN, HYENA_ORDER, 2, HYENA_WIDTH).transpose(2, 0, 1, 3).reshape(
        2, HYENA_HIDDEN, HYENA_ORDER * HYENA_WIDTH).astype(BF16)
    deltas = np.abs(np.linspace(math.log(HYENA_DECAY_TARGET) / HYENA_LONG_DECAY_PCT,
                                math.log(HYENA_DECAY_TARGET) / HYENA_SHORT_DECAY_PCT, HYENA_WIDTH))
    half = n_tiles // 2
    small = lambda a: pl.BlockSpec(a.shape, lambda i: (0,) * a.ndim)
    b1r, b2r, frr = b1.reshape(1, -1), b2.reshape(1, -1), freq.reshape(1, -1)
    dl = jnp.asarray(deltas.reshape(1, -1), F32)
    if t_layout:
        taps_shape = (HYENA_ORDER, tm // SUB, n_tiles * SUB, HYENA_WIDTH)
        taps_spec = pl.BlockSpec((HYENA_ORDER, tm // SUB, SUB, HYENA_WIDTH), lambda i: (0, 0, i, 0))
    else:
        taps_shape = (HYENA_ORDER, 2 * n, HYENA_WIDTH)
        taps_spec = pl.BlockSpec((HYENA_ORDER, tm, HYENA_WIDTH), lambda i: (0, i, 0))
    return pl.pallas_call(
        functools.partial(_filter_kernel, tm=tm, zero_tile=half), grid=(n_tiles,),
        in_specs=[pl.BlockSpec((tm, HYENA_HIDDEN), lambda i: (i, 0)), small(w1p), small(b1r), small(w2), small(b2r),
                  small(frr), pl.BlockSpec((None, HYENA_HIDDEN, HYENA_ORDER * HYENA_WIDTH), lambda i: (i // half, 0, 0)),
                  small(dl)],
        out_specs=[taps_spec, pl.BlockSpec((HYENA_ORDER, 1, HYENA_WIDTH), lambda i: (0, 0, 0))],
        out_shape=[jax.ShapeDtypeStruct(taps_shape, F32), jax.ShapeDtypeStruct((HYENA_ORDER, 1, HYENA_WIDTH), F32)],
        compiler_params=_params(("arbitrary",)), name="hyena_filter",
    )(z, w1p, b1r, w2, b2r, frr, w3d, dl)


def _dft_mats(n1, n2):
    n = n1 * n2
    a1 = 2 * np.pi * np.outer(np.arange(n1), np.arange(n1)) / n1
    c1, s1 = np.cos(a1), np.sin(a1)
    a2 = 2 * np.pi * np.outer(np.arange(n2), np.arange(n2)) / n2
    c2, s2 = np.cos(a2), np.sin(a2)
    h = n1 // 2
    mats = dict(
        sig=np.block([[c1[:, :h], s1[:, :h]], [-s1[:, :h], c1[:, :h]]]),
        filt=np.concatenate([c1, -s1], axis=0),
        fwd2=np.block([[c2, s2], [-s2, c2]]),
        inv2=np.block([[c2, -s2], [s2, c2]]),
        last=np.block([[c1[:h, :], -s1[:h, :]], [s1[:h, :], c1[:h, :]]]) / n,
    )
    mats = {k: jnp.asarray(v, F32).astype(BF16) for k, v in mats.items()}
    at = (jnp.arange(n1, dtype=jnp.int32)[:, None] * jnp.arange(n2, dtype=jnp.int32)[None, :]).astype(F32) * (2 * math.pi / n)
    tw = tuple(jnp.broadcast_to(f(at)[:, :, None], (n1, n2, LANES)) for f in (jnp.cos, jnp.sin))
    return mats, tw


def _stage1_kernel(m_ref, x_ref, o_ref, *, rows_in, rows_out, complex_in):
    for g in range(FFT_GROUPS):
        for r in range(SUB):
            sl = pl.ds(r, rows_in, stride=SUB)
            if complex_in:
                x = jnp.concatenate([x_ref[0, g, sl, :], x_ref[1, g, sl, :]], axis=0)
            else:
                x = x_ref[g, sl, :]
            o_ref[g, pl.ds(r, rows_out, stride=SUB), :] = _dot(m_ref[...], x.astype(BF16))


def _stage1_call(mat, x, sel, *, complex_in, name):
    w = HYENA_WIDTH
    nl = w // LANES
    _, groups, rows8, _ = x.shape
    if complex_in:
        spec = pl.BlockSpec((2, FFT_GROUPS, rows8, LANES), lambda j, c: (0, j, 0, sel * nl + c))
    else:
        spec = pl.BlockSpec((None, FFT_GROUPS, rows8, LANES), lambda j, c: (sel, j, 0, c))
    rows_out = mat.shape[0]
    kern = functools.partial(_stage1_kernel, rows_in=rows8 // SUB, rows_out=rows_out, complex_in=complex_in)
    return pl.pallas_call(
        kern, grid=(groups // FFT_GROUPS, nl),
        in_specs=[pl.BlockSpec(mat.shape, lambda j, c: (0, 0)), spec],
        out_specs=pl.BlockSpec((FFT_GROUPS, rows_out * SUB, LANES), lambda j, c: (j, 0, c)),
        out_shape=jax.ShapeDtypeStruct((groups, rows_out * SUB, w), F32),
        compiler_params=_params(("parallel", "parallel")), name=name,
    )(mat, x)


def _lane_tile(t):
    return jnp.concatenate([t] * (HYENA_WIDTH // LANES), axis=-1)


def _twiddle_dft2(ar, ai, tc, ts, f_ref, n2):
    br = ar * tc + ai * ts
    bi = ai * tc - ar * ts
    x = _dot(f_ref[...], jnp.concatenate([br, bi], axis=0).astype(BF16))
    return x[:n2], x[n2:]


def _mid_kernel(f_ref, g_ref, ar_ref, ai_ref, fr_ref, fi_ref, tc_ref, ts_ref, ss_ref, or_ref, oi_ref, *, n2):
    tc, ts = _lane_tile(tc_ref[...]), _lane_tile(ts_ref[...])
    w = ar_ref.shape[-1]
    tc2, ts2 = jnp.concatenate([tc, tc], axis=1), jnp.concatenate([ts, ts], axis=1)
    ar = jnp.concatenate([ar_ref[...].reshape(n2, w), fr_ref[...].reshape(n2, w)], axis=1)
    ai = jnp.concatenate([ai_ref[...].reshape(n2, w), fi_ref[...].reshape(n2, w)], axis=1)
    xr, xi = _twiddle_dft2(ar, ai, tc2, ts2, f_ref, n2)
    norm = lax.rsqrt(ss_ref[...] + EPS)
    hr, hi = xr[:, w:] * norm, xi[:, w:] * norm
    xr, xi = xr[:, :w], xi[:, :w]
    yr = xr * hr - xi * hi
    yi = xr * hi + xi * hr
    y = _dot(g_ref[...], jnp.concatenate([yr, yi], axis=0).astype(BF16))
    cr, ci = y[:n2], y[n2:]
    or_ref[...] = (cr * tc - ci * ts).reshape(or_ref.shape)
    oi_ref[...] = (ci * tc + cr * ts).reshape(oi_ref.shape)


def _mid_call(fwd2, inv2, a, af, twc, tws, ss, order):
    groups, rows8, w = a.shape
    n2, n1 = groups * SUB, rows8 // (2 * SUB)
    col = lambda off: pl.BlockSpec((groups, SUB, w), lambda k: (0, k + off, 0))
    tw = pl.BlockSpec((None, n2, LANES), lambda k: (k, 0, 0))
    mat = pl.BlockSpec(fwd2.shape, lambda k: (0, 0))
    return pl.pallas_call(
        functools.partial(_mid_kernel, n2=n2), grid=(n1,),
        in_specs=[mat, mat, col(0), col(n1), col(0), col(n1), tw, tw,
                  pl.BlockSpec((None, 1, w), lambda k: (order, 0, 0))],
        out_specs=[col(0)] * 2,
        out_shape=[jax.ShapeDtypeStruct((groups, n1 * SUB, w), F32)] * 2,
        compiler_params=_params(("parallel",)), name="hyena_conv_mid",
    )(fwd2, inv2, a, a, af, af, twc, tws, ss)


def _last_kernel(m_ref, br_ref, bi_ref, z_ref, x_ref, skip_ref, o_ref, *, n1):
    skip = skip_ref[...]
    half = n1 // 2
    for g in range(FFT_GROUPS):
        for r in range(SUB):
            full, part = pl.ds(r, n1, stride=SUB), pl.ds(r, half, stride=SUB)
            y = _dot(m_ref[...], jnp.concatenate([br_ref[g, full, :], bi_ref[g, full, :]], axis=0).astype(BF16))
            o_ref[0, g, part, :] = x_ref[0, g, part, :] * (y[:half] + z_ref[0, g, part, :] * skip)
            o_ref[1, g, part, :] = x_ref[1, g, part, :] * (y[half:] + z_ref[1, g, part, :] * skip)


def _last_call(mat, br, bi, zsrc, zcol, xsrc, xcol, skip, order):
    groups, rows8, w = br.shape
    n1 = rows8 // SUB
    nl = w // LANES
    b_spec = pl.BlockSpec((FFT_GROUPS, rows8, LANES), lambda j, c: (j, 0, c))
    seq = lambda col: pl.BlockSpec((2, FFT_GROUPS, rows8 // 2, LANES), lambda j, c: (0, j, 0, col * nl + c))
    return pl.pallas_call(
        functools.partial(_last_kernel, n1=n1), grid=(groups // FFT_GROUPS, nl),
        in_specs=[pl.BlockSpec(mat.shape, lambda j, c: (0, 0)), b_spec, b_spec, seq(zcol), seq(xcol),
                  pl.BlockSpec((None, 1, LANES), lambda j, c: (order, 0, c))],
        out_specs=seq(0),
        out_shape=jax.ShapeDtypeStruct((2, groups, rows8 // 2, w), F32),
        compiler_params=_params(("parallel", "parallel")), name="hyena_conv_last",
    )(mat, br, bi, zsrc, xsrc, skip)


def _hyena_long(uct, taps_t, ss, skip):
    n2 = uct.shape[1] * SUB
    mats, (twc, tws) = _dft_mats(FFT_N1, n2)
    skip3 = skip.reshape(HYENA_ORDER, 1, HYENA_WIDTH)
    z, zcol = uct, 0
    for o in range(HYENA_ORDER):
        af = _stage1_call(mats["filt"], taps_t, o, complex_in=False, name="hyena_filter_dft1")
        a = _stage1_call(mats["sig"], z, zcol, complex_in=True, name="hyena_conv_first")
        br, bi = _mid_call(mats["fwd2"], mats["inv2"], a, af, twc, tws, ss, o)
        z = _last_call(mats["last"], br, bi, z, zcol, uct, o + 1, skip3, o)
        zcol = 0
    return z


def _ctx_hyena_kernel(v_ref, x1_ref, x2_ref, taps_ref, ss_ref, skip_ref, fh_ref, fz_ref, fi_ref, o_ref, *, n):
    nn = 2 * n
    stack = lambda r: jnp.concatenate([r[0], r[1]], axis=0)
    z = stack(v_ref)
    for o, x_ref in enumerate((x1_ref, x2_ref)):
        h = _dot(fh_ref[...], taps_ref[o], HIGHEST) * lax.rsqrt(ss_ref[o] + EPS)
        hr, hi = h[:nn], h[nn:]
        zf = _dot(fz_ref[...], z, HIGHEST)
        zr, zi = zf[:nn], zf[nn:]
        y = _dot(fi_ref[...], jnp.concatenate([zr * hr - zi * hi, zr * hi + zi * hr], axis=0), HIGHEST)
        z = stack(x_ref) * (y + z * skip_ref[o])
    o_ref[0] = z[:n].astype(o_ref.dtype)
    o_ref[1] = z[n:].astype(o_ref.dtype)


def _ctx_hyena_call(u, taps, ss, skip):
    _, n, _ = u.shape
    nn = 2 * n
    ang = 2 * np.pi * np.outer(np.arange(nn), np.arange(nn)) / nn
    c, s = np.cos(ang), np.sin(ang)
    fh = jnp.asarray(np.concatenate([c, -s], axis=0), F32)
    fz = jnp.asarray(np.block([[c[:, :n], s[:, :n]], [-s[:, :n], c[:, :n]]]), F32)
    fi = jnp.asarray(np.block([[c[:n, :], -s[:n, :]], [s[:n, :], c[:n, :]]]) / nn, F32)
    nchunk = HYENA_WIDTH // LANES
    skip3 = skip.reshape(HYENA_ORDER, 1, HYENA_WIDTH)
    small = lambda a: pl.BlockSpec(a.shape, lambda j: (0,) * a.ndim)
    part = lambda g: pl.BlockSpec((2, n, LANES), lambda j, g=g: (0, 0, g * nchunk + j))

    return pl.pallas_call(
        functools.partial(_ctx_hyena_kernel, n=n), grid=(nchunk,),
        in_specs=[part(0), part(1), part(2),
                  pl.BlockSpec((HYENA_ORDER, nn, LANES), lambda j: (0, 0, j)),
                  pl.BlockSpec((HYENA_ORDER, 1, LANES), lambda j: (0, 0, j)),
                  pl.BlockSpec((HYENA_ORDER, 1, LANES), lambda j: (0, 0, j)),
                  small(fh), small(fz), small(fi)],
        out_specs=pl.BlockSpec((2, n, LANES), lambda j: (0, 0, j)),
        out_shape=jax.ShapeDtypeStruct((2, n, HYENA_WIDTH), BF16),
        compiler_params=_params(("parallel",)), name="hyena_ctx",
    )(u, u, u, taps, ss, skip3, fh, fz, fi)


def _outproj_kernel(x_ref, ya_ref, yb_ref, g_ref, w_ref, gate_ref, o_ref):
    tm, half = ya_ref.shape
    ya = (ya_ref[...].astype(F32) * g_ref[:, :half].astype(F32)).astype(BF16)
    yb = (yb_ref[...].reshape(tm, half).astype(F32) * g_ref[:, half:].astype(F32)).astype(BF16)
    upd = _dot(ya, w_ref[:half, :]) + _dot(yb, w_ref[half:, :])
    o_ref[...] = x_ref[...] + gate_ref[...] * upd


def _outproj_call(x2d, ya, yb, gates, w_bf, gate_vec, *, seq_len, tm, per_batch_mod, yb_t_layout=False):
    r, d = x2d.shape
    tps = seq_len // tm
    half = ya.shape[-1]
    if yb_t_layout:
        yb_spec = pl.BlockSpec((None, tm // SUB, SUB, half), lambda i: (i // tps, 0, i % tps, 0))
    else:
        yb_spec = pl.BlockSpec((tm, half), lambda i: (i, 0))
    mod_map = (lambda i: (i // tps, 0, 0)) if per_batch_mod else (lambda i: (0, 0, 0))
    return pl.pallas_call(
        _outproj_kernel, grid=(r // tm,),
        in_specs=[pl.BlockSpec((tm, d), lambda i: (i, 0)), pl.BlockSpec((tm, half), lambda i: (i, 0)), yb_spec,
                  pl.BlockSpec((tm, 2 * half), lambda i: (i, 0)), pl.BlockSpec(w_bf.shape, lambda i: (0, 0)),
                  pl.BlockSpec((None, 1, d), mod_map)],
        out_specs=pl.BlockSpec((tm, d), lambda i: (i, 0)),
        out_shape=jax.ShapeDtypeStruct((r, d), F32),
        compiler_params=_params(("parallel",)), name="outproj",
    )(x2d, ya, yb, gates, w_bf, gate_vec)


def _rope_tables(n_tokens):
    t = jnp.arange(n_tokens, dtype=jnp.int32)
    row = (t // GRID_W).astype(F32)
    col = (t % GRID_W).astype(F32)
    n_pairs = HEAD_DIM // 4
    inv = ROPE_THETA ** (-jnp.arange(n_pairs, dtype=F32) / n_pairs)
    ang = jnp.concatenate([row[:, None] * inv, col[:, None] * inv], axis=-1)
    cos = jnp.repeat(jnp.cos(ang), 2, axis=-1)
    sin = jnp.repeat(jnp.sin(ang), 2, axis=-1)
    even = (jnp.arange(HEAD_DIM) % 2 == 0)[None, :]
    tile = lambda a: jnp.tile(a, (1, LANES // HEAD_DIM))
    return tile(cos), tile(jnp.where(even, -sin, 0.0)), tile(jnp.where(even, 0.0, sin))


def _gain_rows(*gains):
    rows = [jnp.tile(g, LANES // HEAD_DIM) for g in gains]
    rows += [jnp.zeros((LANES,), F32)] * (8 - len(rows))
    return jnp.stack(rows)


def kernel(x, c, ctx, c_ctx, norm_g, w_ada, b_ada, w_in, w_out, glob_q_norm, glob_k_norm, hy_conv_w, hy_conv_b, hy_w1, hy_b1, hy_w2, hy_b2, hy_freq, hy_w3, hy_skip, win_q_norm, win_k_norm, win_sink, nat_q_norm, nat_k_norm, nat_rpb):
    b, s, d = x.shape
    n_ctx = ctx.shape[1]
    n2 = 2 * s // FFT_N1
    tm = n2
    w_in_bf, w_out_bf = w_in.astype(BF16), w_out.astype(BF16)

    cond = jnp.zeros((8, d), F32).at[:b].set(c).at[b].set(c_ctx)
    mod = _ada_call(cond, w_ada, b_ada)
    shift, scale, gate = (mod[:, :, i * d:(i + 1) * d] for i in range(3))
    lat = lambda m, l: m[l, :b].reshape(b, 1, d)
    cx = lambda m, l: m[l, b:b + 1].reshape(1, 1, d)

    rope = _rope_tables(s)
    x2 = x.reshape(b * s, d)
    xc2 = ctx.reshape(b * n_ctx, d)
    heads4 = lambda a, n: a.reshape(a.shape[0], b, n, a.shape[-1])

    gains0 = _gain_rows(glob_q_norm[0], glob_k_norm[0])
    gw, kw = GROUP_WIDTH, N_KV_HEADS * HEAD_DIM
    hyw = (HYENA_ORDER + 1) * HYENA_WIDTH
    plan_lat = (("qk", 0, gw, 0, True, QK_SCALE2), ("qk", gw, kw, 1, True, 1.0), ("vones", gw + kw, kw),
                ("hy", gw + 2 * kw, hyw), ("gate", gw + 2 * kw + hyw, 2 * gw))
    plan_ctx = (("qk", 0, gw, 0, False, QK_SCALE2), ("qk", gw, kw, 1, False, 1.0), ("vones", gw + kw, kw),
                ("hy", gw + 2 * kw, hyw), ("gate", gw + 2 * kw + hyw, 2 * gw))
    conv = (hy_conv_w[0], hy_conv_b[0])
    q0, k0, v0, uct, g0 = _inproj_call(
        x2, lat(shift, 0), lat(scale, 0), norm_g[0], w_in_bf[0], plan_lat, seq_len=s, tm=tm, per_batch_mod=True,
        rope_tabs=rope, gains=gains0, conv=conv, hy_t_layout=True, name="inproj_even")
    qc, kc, vc, ucc, gc = _inproj_call(
        xc2, cx(shift, 0), cx(scale, 0), norm_g[0], w_in_bf[0], plan_ctx, seq_len=n_ctx, tm=n_ctx,
        per_batch_mod=False, gains=gains0, conv=conv, name="inproj_even_ctx")
    kc4, vc4 = heads4(kc, n_ctx), heads4(vc, n_ctx)
    k_all = jnp.concatenate([kc4, heads4(k0, s)], axis=2)
    v_all = jnp.concatenate([vc4, heads4(v0, s)], axis=2)
    ya = _flash_call(heads4(q0, s), k_all, v_all, tq=256, name="global_gqa")
    yca = _flash_call(heads4(qc, n_ctx), kc4, vc4, tq=n_ctx, name="ctx_gqa")

    filt = (hy_w1[0], hy_b1[0], hy_w2[0], hy_b2[0], hy_freq[0], hy_w3[0])
    taps_t, ss = _filter_call(s, *filt, tm=n2, t_layout=True)
    ybt = _hyena_long(uct, taps_t, ss, hy_skip[0])
    taps_c, ss_c = _filter_call(n_ctx, *filt, tm=n_ctx, t_layout=False)
    ycb = _ctx_hyena_call(ucc.reshape(b, n_ctx, hyw), taps_c, ss_c, hy_skip[0])

    x2 = _outproj_call(x2, ya.reshape(b * s, gw), ybt, g0, w_out_bf[0], lat(gate, 0), seq_len=s, tm=tm,
                       per_batch_mod=True, yb_t_layout=True)
    xc2 = _outproj_call(xc2, yca.reshape(b * n_ctx, gw), ycb.reshape(b * n_ctx, gw), gc, w_out_bf[0], cx(gate, 0),
                        seq_len=n_ctx, tm=n_ctx, per_batch_mod=False)

    gains1 = _gain_rows(win_q_norm[0], win_k_norm[0], nat_q_norm[0], nat_k_norm[0])
    o_nq = gw + 2 * kw
    plan_lat = (("qk", 0, gw, 0, True, QK_SCALE2), ("qk", gw, kw, 1, True, 1.0), ("v", gw + kw, kw),
                ("qk", o_nq, gw, 2, False, QK_SCALE2), ("qk", o_nq + gw, gw, 3, False, 1.0), ("v", o_nq + 2 * gw, gw),
                ("gate", o_nq + 3 * gw, 2 * gw))
    plan_ctx = (("qk", gw, kw, 1, False, 1.0), ("v", gw + kw, kw),
                ("qk", o_nq + gw, gw, 3, False, 1.0), ("v", o_nq + 2 * gw, gw))
    qw, kwl, vwl, qn, kn, vn, g1 = _inproj_call(
        x2, lat(shift, 1), lat(scale, 1), norm_g[1], w_in_bf[1], plan_lat, seq_len=s, tm=tm, per_batch_mod=True,
        rope_tabs=rope, gains=gains1, name="inproj_odd")
    kwc, vwc, knc, vnc = _inproj_call(
        xc2, cx(shift, 1), cx(scale, 1), norm_g[1], w_in_bf[1], plan_ctx, seq_len=n_ctx, tm=n_ctx,
        per_batch_mod=False, gains=gains1, name="inproj_odd_ctx")
    yw = _window_call(win_sink[0], heads4(qw, s), heads4(kwc, n_ctx), heads4(vwc, n_ctx), heads4(kwl, s), heads4(vwl, s))
    table = _rpb_table_call(nat_rpb[0])
    yn = _na_call(heads4(qn, s), heads4(kn, s), heads4(vn, s), heads4(knc, n_ctx), heads4(vnc, n_ctx), table)
    x2 = _outproj_call(x2, yw.reshape(b * s, gw), yn.reshape(b * s, gw), g1, w_out_bf[1], lat(gate, 1), seq_len=s,
                       tm=tm, per_batch_mod=True)
    return x2.reshape(b, s, d)
```
